```python
import jax, jax.numpy as jnp
from jax import lax
import numpy as np

D_MODEL = 1024
BATCH = 16
SEQ = 2048
DEPTH = 1

GRID_W = 64
CTX_LEN = 256
MIX_WIDTH = D_MODEL
POOL_WIDTH = MIX_WIDTH // 2
POOL_WINDOWS = (2, 4, 8, 16)
POOL_GROUP = POOL_WIDTH // len(POOL_WINDOWS)
MLA_HEADS = 8
QK_NOPE = 64
QK_ROPE = 32
QK_HEAD = QK_NOPE + QK_ROPE
V_HEAD = 64
Q_LORA = 384
KV_LORA = 256
ROPE_THETA = 10000.0
ATTN_BLOCK = 128
IN_WIDTH = POOL_WIDTH + Q_LORA + KV_LORA + QK_ROPE
PEER_HEADS = 8
PEER_N_KEYS = 128
PEER_N_EXPERTS = PEER_N_KEYS * PEER_N_KEYS
PEER_QUERY_DIM = 256
PEER_HALF = PEER_QUERY_DIM // 2
PEER_TOPK = 16
PEER_BLOCK = 128
EPS = 1e-6

kernel_name = "hymba_pool_mla_peer_dit_layer"


def rms_norm(x, g):
    xf = x.astype(jnp.float32)
    y = xf * lax.rsqrt(jnp.mean(xf * xf, axis=-1, keepdims=True) + EPS)
    return (y * g.astype(jnp.float32)).astype(x.dtype)


def modulate(h, shift, scale):
    return h * (1.0 + scale[..., None, :]) + shift[..., None, :]


def axial_rope_tables(n_tokens):
    rows = n_tokens // GRID_W
    row = jnp.repeat(jnp.arange(rows), GRID_W).astype(jnp.float32)
    col = jnp.tile(jnp.arange(GRID_W), rows).astype(jnp.float32)
    n = QK_ROPE // 4
    inv = 1.0 / (ROPE_THETA ** (jnp.arange(n, dtype=jnp.float32) / n))
    ang_r = row[:, None] * inv
    ang_c = col[:, None] * inv
    ang = jnp.concatenate([ang_r, ang_r, ang_c, ang_c], axis=-1)
    return jnp.cos(ang), jnp.sin(ang)


def apply_rope(t, cos, sin):
    nope, rope = t[..., :QK_NOPE], t[..., QK_NOPE:]
    n = QK_ROPE // 4
    r = rope.reshape(rope.shape[:-1] + (2, 2, n))
    rot = jnp.stack([-r[..., 1, :], r[..., 0, :]], axis=-2).reshape(rope.shape)
    rope = rope * cos[:, None, :] + rot * sin[:, None, :]
    return jnp.concatenate([nope, rope.astype(t.dtype)], axis=-1)


def multiscale_pool(p, pool_w, pool_scale):
    B, L, C = p.shape
    pf = p.astype(jnp.float32)
    cs = jnp.concatenate([jnp.zeros((B, 1, C), jnp.float32), jnp.cumsum(pf, axis=1)], axis=1)
    t = jnp.arange(L)
    outs = []
    for g, w in enumerate(POOL_WINDOWS):
        lo = jnp.clip(t - w // 2, 0, L)
        hi = jnp.clip(t + w // 2, 0, L)
        sl = slice(g * POOL_GROUP, (g + 1) * POOL_GROUP)
        csg = cs[..., sl]
        mean = (jnp.take(csg, hi, axis=1) - jnp.take(csg, lo, axis=1)) / (hi - lo).astype(jnp.float32)[:, None]
        outs.append(jnp.einsum('blc,cd->bld', mean - pf[..., sl], pool_w[g].astype(jnp.float32)))
    y = jnp.concatenate(outs, axis=-1) * pool_scale.astype(jnp.float32)
    return y.astype(p.dtype)


def mla_keys_values(ckv, kr, g_kv_lora, w_kv_up, g_qk_k):
    B, L, _ = ckv.shape
    kv = (rms_norm(ckv, g_kv_lora) @ w_kv_up).reshape(B, L, MLA_HEADS, QK_NOPE + V_HEAD)
    k_nope, v = kv[..., :QK_NOPE], kv[..., QK_NOPE:]
    k = jnp.concatenate([k_nope, jnp.broadcast_to(kr[:, :, None, :], (B, L, MLA_HEADS, QK_ROPE))], axis=-1)
    return rms_norm(k, g_qk_k), v


def block_attention(q, k_all, v_all):
    B, S, H, _ = q.shape
    nb = S // ATTN_BLOCK
    qb = q.reshape(B, nb, ATTN_BLOCK, H, QK_HEAD).transpose(1, 0, 3, 2, 4)
    kt = k_all.transpose(0, 2, 1, 3)
    vt = v_all.transpose(0, 2, 1, 3)
    scale = 1.0 / np.sqrt(QK_HEAD)

    def one_block(qblk):
        s = jnp.einsum('bhqd,bhkd->bhqk', qblk, kt).astype(jnp.float32) * scale
        pr = jax.nn.softmax(s, axis=-1)
        return jnp.einsum('bhqk,bhkd->bhqd', pr.astype(vt.dtype), vt)

    o = lax.map(one_block, qb)
    return o.transpose(1, 0, 3, 2, 4).reshape(B, S, H * V_HEAD)


def peer_ffn(h, peer_w_q, peer_sub_keys, peer_u, peer_v):
    B, L, D = h.shape
    qp = (h @ peer_w_q).reshape(B, L, PEER_HEADS, 2, PEER_HALF)
    s_half = jnp.einsum('blhpd,pnd->blhpn', qp, peer_sub_keys).astype(jnp.float32)
    v1, i1 = lax.top_k(s_half[..., 0, :], PEER_TOPK)
    v2, i2 = lax.top_k(s_half[..., 1, :], PEER_TOPK)
    cand = (v1[..., :, None] + v2[..., None, :]).reshape(B, L, PEER_HEADS, PEER_TOPK * PEER_TOPK)
    cidx = (i1[..., :, None] * PEER_N_KEYS + i2[..., None, :]).reshape(B, L, PEER_HEADS, PEER_TOPK * PEER_TOPK)
    top, pos = lax.top_k(cand, PEER_TOPK)
    eidx = jnp.take_along_axis(cidx, pos, axis=-1)
    gates = jax.nn.softmax(top, axis=-1)
    T = B * L
    nblk = T // PEER_BLOCK
    xt = h.reshape(nblk, PEER_BLOCK, D)
    et = eidx.reshape(nblk, PEER_BLOCK, PEER_HEADS * PEER_TOPK)
    gt = gates.reshape(nblk, PEER_BLOCK, PEER_HEADS * PEER_TOPK)

    def expert_block(args):
        xb, eb, gb = args
        u = jnp.take(peer_u, eb, axis=0)
        vv = jnp.take(peer_v, eb, axis=0)
        a = jax.nn.gelu(jnp.einsum('td,ted->te', xb, u).astype(jnp.float32), approximate=False)
        return jnp.einsum('te,ted->td', (gb * a).astype(xb.dtype), vv)

    y = lax.map(expert_block, (xt, et, gt))
    return y.reshape(B, L, D)


def setup_inputs(seed: int = 0) -> dict:
    key = jax.random.key(seed)
    ks = jax.random.split(key, 22)
    f = jnp.float32
    nrm = lambda k, shape, s: jax.random.normal(k, shape, f) * s
    return {
        "x": nrm(ks[0], (BATCH, SEQ, D_MODEL), 1.0),
        "c": nrm(ks[1], (BATCH, D_MODEL), 1.0),
        "ctx": nrm(ks[2], (BATCH, CTX_LEN, D_MODEL), 1.0),
        "c_ctx": nrm(ks[3], (D_MODEL,), 1.0),
        "w_ada": nrm(ks[4], (D_MODEL, 6 * D_MODEL), 0.5 * D_MODEL ** -0.5),
        "b_ada": nrm(ks[5], (6 * D_MODEL,), 0.02),
        "g_norm1": 1.0 + nrm(ks[6], (D_MODEL,), 0.02),
        "w_in": nrm(ks[7], (D_MODEL, IN_WIDTH), D_MODEL ** -0.5),
        "pool_w": nrm(ks[8], (len(POOL_WINDOWS), POOL_GROUP, POOL_GROUP), POOL_GROUP ** -0.5),
        "pool_scale": 1.0 + nrm(ks[9], (POOL_WIDTH,), 0.1),
        "g_q_lora": 1.0 + nrm(ks[10], (Q_LORA,), 0.02),
        "w_q_up": nrm(ks[11], (Q_LORA, MLA_HEADS * QK_HEAD), Q_LORA ** -0.5),
        "g_kv_lora": 1.0 + nrm(ks[12], (KV_LORA,), 0.02),
        "w_kv_up": nrm(ks[13], (KV_LORA, MLA_HEADS * (QK_NOPE + V_HEAD)), KV_LORA ** -0.5),
        "g_qk_q": 1.0 + nrm(ks[14], (QK_HEAD,), 0.02),
        "g_qk_k": 1.0 + nrm(ks[15], (QK_HEAD,), 0.02),
        "w_out": nrm(ks[16], (MIX_WIDTH, D_MODEL), MIX_WIDTH ** -0.5),
        "g_norm2": 1.0 + nrm(ks[17], (D_MODEL,), 0.02),
        "peer_w_q": nrm(ks[18], (D_MODEL, PEER_HEADS * PEER_QUERY_DIM), D_MODEL ** -0.5),
        "peer_sub_keys": nrm(ks[19], (2, PEER_N_KEYS, PEER_HALF), PEER_HALF ** -0.5),
        "peer_u": nrm(ks[20], (PEER_N_EXPERTS, D_MODEL), D_MODEL ** -0.5),
        "peer_v": nrm(ks[21], (PEER_N_EXPERTS, D_MODEL), 0.5),
    }


def hybrid_layer(x, c, ctx, c_ctx, w_ada, b_ada, g_norm1, w_in, pool_w, pool_scale,
                 g_q_lora, w_q_up, g_kv_lora, w_kv_up, g_qk_q, g_qk_k, w_out,
                 g_norm2, peer_w_q, peer_sub_keys, peer_u, peer_v):
    B, S, D = x.shape
    mod = jax.nn.silu(c) @ w_ada + b_ada
    shift1, scale1, gate1, shift2, scale2, gate2 = jnp.split(mod, 6, axis=-1)
    mod_ctx = jax.nn.silu(c_ctx) @ w_ada[:, :2 * D] + b_ada[:2 * D]
    shift1_c, scale1_c = mod_ctx[:D], mod_ctx[D:]

    h1 = modulate(rms_norm(x, g_norm1), shift1, scale1)
    proj = h1 @ w_in
    o1 = POOL_WIDTH
    o2 = o1 + Q_LORA
    o3 = o2 + KV_LORA
    p_in, cq, ckv, kr = proj[..., :o1], proj[..., o1:o2], proj[..., o2:o3], proj[..., o3:]

    pool_out = multiscale_pool(p_in, pool_w, pool_scale)

    cos, sin = axial_rope_tables(S)
    q = (rms_norm(cq, g_q_lora) @ w_q_up).reshape(B, S, MLA_HEADS, QK_HEAD)
    q = apply_rope(rms_norm(q, g_qk_q), cos, sin)
    k_lat, v_lat = mla_keys_values(ckv, kr, g_kv_lora, w_kv_up, g_qk_k)
    k_lat = apply_rope(k_lat, cos, sin)
    hc = modulate(rms_norm(ctx, g_norm1), shift1_c, scale1_c)
    projc = hc @ w_in[:, o2:]
    k_ctx, v_ctx = mla_keys_values(projc[..., :KV_LORA], projc[..., KV_LORA:], g_kv_lora, w_kv_up, g_qk_k)
    k_all = jnp.concatenate([k_lat, k_ctx], axis=1)
    v_all = jnp.concatenate([v_lat, v_ctx], axis=1)
    mla_out = block_attention(q, k_all, v_all)

    mix = jnp.concatenate([pool_out, mla_out], axis=-1) @ w_out
    x = x + gate1[:, None, :] * mix

    h2 = modulate(rms_norm(x, g_norm2), shift2, scale2)
    x = x + gate2[:, None, :] * peer_ffn(h2, peer_w_q, peer_sub_keys, peer_u, peer_v)
    return x


def reference(x, c, ctx, c_ctx, w_ada, b_ada, g_norm1, w_in, pool_w, pool_scale,
              g_q_lora, w_q_up, g_kv_lora, w_kv_up, g_qk_q, g_qk_k, w_out,
              g_norm2, peer_w_q, peer_sub_keys, peer_u, peer_v):
    for _ in range(DEPTH):
        x = hybrid_layer(x, c, ctx, c_ctx, w_ada, b_ada, g_norm1, w_in, pool_w, pool_scale,
                         g_q_lora, w_q_up, g_kv_lora, w_kv_up, g_qk_q, g_qk_k, w_out,
                         g_norm2, peer_w_q, peer_sub_keys, peer_u, peer_v)
    return x
```

```python
import functools
import math

import jax
import jax.numpy as jnp
import numpy as np
from jax import lax
from jax.experimental import pallas as pl
from jax.experimental.pallas import tpu as pltpu

GRID_W = 64
POOL_WINDOWS = (2, 4, 8, 16)
POOL_GROUP = 128
POOL_WIDTH = POOL_GROUP * len(POOL_WINDOWS)
MLA_HEADS = 8
QK_NOPE = 64
QK_ROPE = 32
QK_HEAD = QK_NOPE + QK_ROPE
V_HEAD = 64
Q_LORA = 384
KV_LORA = 256
ROPE_THETA = 10000.0
PEER_HEADS = 8
PEER_N_KEYS = 128
PEER_HALF = 128
PEER_TOPK = 16
EPS = 1e-6

LANE = 128
N_MOD_ROWS = 8
N_RANK = PEER_TOPK + 1
VMEM_LIMIT = 48 * 1024 * 1024

BF16 = jnp.bfloat16
F32 = jnp.float32


def _params(*sem):
    return pltpu.CompilerParams(dimension_semantics=sem, vmem_limit_bytes=VMEM_LIMIT)


def _dot(a, b):
    return jnp.dot(a, b, preferred_element_type=F32)


def _split3(a):
    hi = a.astype(BF16)
    lo = (a - hi.astype(F32)).astype(BF16)
    return hi, lo


def _rms(xf, g):
    ms = jnp.mean(xf * xf, axis=-1, keepdims=True)
    return xf * lax.rsqrt(ms + EPS) * g


def _ada_kernel(c_ref, w_ref, b_ref, o_ref):
    c = c_ref[...]
    a = c / (1.0 + jnp.exp(-c))
    a_hi, a_lo = _split3(a)
    w_hi, w_lo = _split3(w_ref[...])
    o_ref[...] = _dot(a_hi, w_hi) + _dot(a_hi, w_lo) + _dot(a_lo, w_hi) + b_ref[...]


def _ada(cvec, w_ada, b_ada):
    rows, d = cvec.shape
    n = w_ada.shape[1]
    tn = 1536
    return pl.pallas_call(
        _ada_kernel,
        grid=(n // tn,),
        in_specs=[pl.BlockSpec((rows, d), lambda j: (0, 0)),
                  pl.BlockSpec((d, tn), lambda j: (0, j)),
                  pl.BlockSpec((1, tn), lambda j: (0, j))],
        out_specs=pl.BlockSpec((rows, tn), lambda j: (0, j)),
        out_shape=jax.ShapeDtypeStruct((rows, n), F32),
        compiler_params=_params("arbitrary"),
        name="ada",
    )(cvec, w_ada, b_ada.reshape(1, n))


def _head_norm_rope(slab, table):
    lane = lax.broadcasted_iota(jnp.int32, slab.shape, 1)
    ss = jnp.sum(jnp.where(lane < QK_HEAD, slab * slab, 0.0), axis=-1, keepdims=True)
    t = slab * lax.rsqrt(ss * (1.0 / QK_HEAD) + EPS) * table
    r = pltpu.roll(t, LANE - QK_ROPE, axis=1)
    return jnp.where(lane < QK_NOPE, t, jnp.where(lane < QK_HEAD, t + r, 0.0))


def _modulated(x_ref, mod_ref, gn_ref):
    h = _rms(x_ref[...], gn_ref[...])
    return (h * (1.0 + mod_ref[0, 1:2, :]) + mod_ref[0, 0:1, :]).astype(BF16)


def _keys_values(ckv, krx, gkv_ref, wkv_ref, tk, k_ref, v_ref):
    kvx = _dot(_rms(ckv, gkv_ref[...]).astype(BF16), wkv_ref[...])
    hw = MLA_HEADS * LANE
    for h in range(MLA_HEADS):
        slab = kvx[:, h * LANE:(h + 1) * LANE] + krx
        k_ref[:, h * LANE:(h + 1) * LANE] = _head_norm_rope(slab, tk).astype(BF16)
    v_ref[...] = kvx[:, hw:2 * hw].astype(BF16)


def _proj_kernel(x_ref, mod_ref, gn_ref, win_ref, gq_ref, wq_ref, gkv_ref, wkv_ref, tq_ref, tk_ref,
                 p_ref, q_ref, k_ref, v_ref):
    proj = _dot(_modulated(x_ref, mod_ref, gn_ref), win_ref[...])
    o1, o2, o3 = POOL_WIDTH, POOL_WIDTH + Q_LORA, POOL_WIDTH + Q_LORA + KV_LORA
    p_ref[...] = proj[:, :o1]
    qs = _dot(_rms(proj[:, o1:o2], gq_ref[...]).astype(BF16), wq_ref[...])
    tq = tq_ref[...]
    for h in range(MLA_HEADS):
        q_ref[:, h * LANE:(h + 1) * LANE] = _head_norm_rope(qs[:, h * LANE:(h + 1) * LANE], tq).astype(BF16)
    _keys_values(proj[:, o2:o3], proj[:, o3:o3 + LANE], gkv_ref, wkv_ref, tk_ref[...], k_ref, v_ref)


def _proj_ctx_kernel(x_ref, mod_ref, gn_ref, win_ref, gkv_ref, wkv_ref, tk_ref, k_ref, v_ref):
    proj = _dot(_modulated(x_ref, mod_ref, gn_ref), win_ref[...])
    _keys_values(proj[:, :KV_LORA], proj[:, KV_LORA:KV_LORA + LANE], gkv_ref, wkv_ref, tk_ref[...], k_ref, v_ref)


def _full(shape):
    return pl.BlockSpec(shape, lambda *_: (0,) * len(shape))


def _proj(x2, mod, gn, win, gq, wq, gkv, wkv, tq, tk, seq, tm):
    t, d = x2.shape
    nps = seq // tm
    hw = MLA_HEADS * LANE
    row = lambda w: pl.BlockSpec((tm, w), lambda i: (i, 0))
    tab = pl.BlockSpec((tm, LANE), lambda i: (i % nps, 0))
    return pl.pallas_call(
        _proj_kernel,
        grid=(t // tm,),
        in_specs=[row(d), pl.BlockSpec((1, N_MOD_ROWS, d), lambda i: (i // nps, 0, 0)), _full(gn.shape),
                  _full(win.shape), _full(gq.shape), _full(wq.shape), _full(gkv.shape), _full(wkv.shape),
                  tab, tab],
        out_specs=[row(POOL_WIDTH), row(hw), row(hw), row(hw)],
        out_shape=[jax.ShapeDtypeStruct((t, POOL_WIDTH), F32), jax.ShapeDtypeStruct((t, hw), BF16),
                   jax.ShapeDtypeStruct((t, hw), BF16), jax.ShapeDtypeStruct((t, hw), BF16)],
        compiler_params=_params("arbitrary"),
        name="proj",
    )(x2, mod, gn, win, gq, wq, gkv, wkv, tq, tk)


def _proj_ctx(c2, modc, gn, win, gkv, wkv, tk, tm):
    t, d = c2.shape
    hw = MLA_HEADS * LANE
    row = lambda w: pl.BlockSpec((tm, w), lambda i: (i, 0))
    return pl.pallas_call(
        _proj_ctx_kernel,
        grid=(t // tm,),
        in_specs=[row(d), _full(modc.shape), _full(gn.shape), _full(win.shape), _full(gkv.shape),
                  _full(wkv.shape), _full(tk.shape)],
        out_specs=[row(hw), row(hw)],
        out_shape=[jax.ShapeDtypeStruct((t, hw), BF16), jax.ShapeDtypeStruct((t, hw), BF16)],
        compiler_params=_params("arbitrary"),
        name="proj_ctx",
    )(c2, modc, gn, win, gkv, wkv, tk)


def _pool_kernel(p_ref, w_ref, sc_ref, o_ref):
    n = p_ref.shape[1]
    t = lax.broadcasted_iota(jnp.int32, (n, POOL_GROUP), 0)

    def shifted(a, d):
        r = pltpu.roll(a, (-d) % n, axis=0)
        return jnp.where((t + d >= 0) & (t + d < n), r, 0.0)

    for g, w in enumerate(POOL_WINDOWS):
        half = w // 2
        sl = slice(g * POOL_GROUP, (g + 1) * POOL_GROUP)
        pg = p_ref[0, :, sl]
        fwd, bwd, k = pg, shifted(pg, -1), 1
        while k < half:
            fwd = fwd + shifted(fwd, k)
            bwd = bwd + shifted(bwd, -k)
            k *= 2
        cnt = (jnp.minimum(t + half, n) - jnp.maximum(t - half, 0)).astype(F32)
        diff = (fwd + bwd) / cnt - pg
        y = _dot(diff.astype(BF16), w_ref[g].astype(BF16)) * sc_ref[:, sl]
        o_ref[0, :, sl] = y.astype(BF16)


def _pool(p3, pool_w, pool_scale):
    b, s, c = p3.shape
    return pl.pallas_call(
        _pool_kernel,
        grid=(b,),
        in_specs=[pl.BlockSpec((1, s, c), lambda i: (i, 0, 0)), _full(pool_w.shape), _full((1, c))],
        out_specs=pl.BlockSpec((1, s, c), lambda i: (i, 0, 0)),
        out_shape=jax.ShapeDtypeStruct((b, s, c), BF16),
        compiler_params=_params("arbitrary"),
        name="pool",
    )(p3, pool_w, pool_scale.reshape(1, c))


def _attn_kernel(q_ref, kl_ref, kc_ref, vl_ref, vc_ref, o_ref):
    nt = (((1,), (1,)), ((), ()))
    q = q_ref[...]
    s1 = lax.dot_general(q, kl_ref[...], nt, preferred_element_type=F32)
    s2 = lax.dot_general(q, kc_ref[...], nt, preferred_element_type=F32)
    m = jnp.maximum(jnp.max(s1, axis=-1, keepdims=True), jnp.max(s2, axis=-1, keepdims=True))
    p1 = jnp.exp(s1 - m)
    p2 = jnp.exp(s2 - m)
    den = jnp.sum(p1, axis=-1, keepdims=True) + jnp.sum(p2, axis=-1, keepdims=True)
    o = _dot(p1.astype(BF16), vl_ref[...]) + _dot(p2.astype(BF16), vc_ref[...])
    o_ref[...] = (o / den).astype(BF16)


def _attn(q, kl, kc, vl, vc, batch, seq, nctx, tq):
    t, hw = q.shape
    nq = seq // tq
    return pl.pallas_call(
        _attn_kernel,
        grid=(batch, MLA_HEADS, nq),
        in_specs=[pl.BlockSpec((tq, LANE), lambda b, h, i: (b * nq + i, h)),
                  pl.BlockSpec((seq, LANE), lambda b, h, i: (b, h)),
                  pl.BlockSpec((nctx, LANE), lambda b, h, i: (b, h)),
                  pl.BlockSpec((seq, LANE), lambda b, h, i: (b, h)),
                  pl.BlockSpec((nctx, LANE), lambda b, h, i: (b, h))],
        out_specs=pl.BlockSpec((tq, LANE), lambda b, h, i: (b * nq + i, h)),
        out_shape=jax.ShapeDtypeStruct((t, hw), BF16),
        compiler_params=_params("arbitrary", "arbitrary", "arbitrary"),
        name="attn",
    )(q, kl, kc, vl, vc)


def _mix_kernel(pool_ref, mla_ref, wp_ref, wm_ref, x_ref, mod_ref, gn_ref, x1_ref, h2t_ref):
    mix = _dot(pool_ref[...], wp_ref[...]) + _dot(mla_ref[...], wm_ref[...])
    x1 = x_ref[...] + mod_ref[0, 2:3, :] * mix
    x1_ref[...] = x1
    h2 = _rms(x1, gn_ref[...]) * (1.0 + mod_ref[0, 4:5, :]) + mod_ref[0, 3:4, :]
    h2t_ref[...] = h2.T.astype(BF16)


def _mix(pool2, mla2, wp, wm, x2, mod, gn, seq, tm):
    t, d = x2.shape
    nps = seq // tm
    row = lambda w: pl.BlockSpec((tm, w), lambda i: (i, 0))
    return pl.pallas_call(
        _mix_kernel,
        grid=(t // tm,),
        in_specs=[row(pool2.shape[1]), row(mla2.shape[1]), _full(wp.shape), _full(wm.shape), row(d),
                  pl.BlockSpec((1, N_MOD_ROWS, d), lambda i: (i // nps, 0, 0)), _full(gn.shape)],
        out_specs=[row(d), pl.BlockSpec((d, tm), lambda i: (0, i))],
        out_shape=[jax.ShapeDtypeStruct((t, d), F32), jax.ShapeDtypeStruct((d, t), BF16)],
        compiler_params=_params("arbitrary"),
        name="mix",
    )(pool2, mla2, wp, wm, x2, mod, gn)


def _staircase():
    return [(a, b) for a in range(N_RANK) for b in range(N_RANK) if (a + 1) * (b + 1) <= N_RANK]


def _pprep_kernel(h2t_ref, wq_ref, keys_ref, a_ref, th_ref, s2_ref, b_ref, qp_scr, top_scr):
    qp_scr[...] = _dot(wq_ref[...], h2t_ref[...]).astype(BF16)
    neg = -jnp.inf
    for h in range(PEER_HEADS):
        for p, dst in ((0, th_ref), (1, s2_ref)):
            row0 = (h * 2 + p) * PEER_HALF
            s = _dot(keys_ref[p], qp_scr[row0:row0 + PEER_HALF, :])
            dst[h] = s
            for r in range(N_RANK):
                m = jnp.max(s, axis=0, keepdims=True)
                top_scr[p, r, h:h + 1, :] = m
                s = jnp.where(s == m, neg, s)
    v1 = [top_scr[0, r] for r in range(N_RANK)]
    v2 = [top_scr[1, r] for r in range(N_RANK)]
    cand = [v1[a] + v2[b] for a, b in _staircase()]
    work = list(cand)
    kth = None
    for r in range(N_RANK):
        prev = kth
        kth = functools.reduce(jnp.maximum, work)
        work = [jnp.where(c == kth, neg, c) for c in work]
    tau = 0.5 * (prev + kth)
    cmax = v1[0] + v2[0]
    den = functools.reduce(lambda x, y: x + y, [jnp.where(c >= tau, jnp.exp(c - cmax), 0.0) for c in cand])
    inv = 1.0 / den
    for h in range(PEER_HEADS):
        hs = slice(h, h + 1)
        s1 = th_ref[h]
        a_ref[h] = jnp.exp(s1 - v1[0][hs]) * inv[hs]
        th_ref[h] = tau[hs] - s1
        b_ref[h] = jnp.exp(s2_ref[h] - v2[0][hs])


def _pprep(h2t, wqt, keys, tn):
    d, t = h2t.shape
    blk = pl.BlockSpec((PEER_HEADS, PEER_N_KEYS, tn), lambda j: (0, 0, j))
    shp = jax.ShapeDtypeStruct((PEER_HEADS, PEER_N_KEYS, t), F32)
    return pl.pallas_call(
        _pprep_kernel,
        grid=(t // tn,),
        in_specs=[pl.BlockSpec((d, tn), lambda j: (0, j)), _full(wqt.shape), _full(keys.shape)],
        out_specs=[blk, blk, blk, blk],
        out_shape=[shp, shp, shp, shp],
        scratch_shapes=[pltpu.VMEM((wqt.shape[0], tn), BF16),
                        pltpu.VMEM((2, N_RANK, PEER_HEADS, tn), F32)],
        compiler_params=_params("arbitrary"),
        name="pprep",
    )(h2t, wqt, keys)


def _pdense_kernel(h2t_ref, u_ref, vt_ref, a_ref, th_ref, s2_ref, b_ref, x1_ref, mod_ref, o_ref,
                   acc_ref, g_ref, *, n_i1):
    e = pl.program_id(1)

    @pl.when(e == 0)
    def _():
        acc_ref[...] = jnp.zeros_like(acc_ref)

    act = _dot(u_ref[...], h2t_ref[...])
    for i in range(n_i1):
        w = None
        for h in range(PEER_HEADS):
            sel = s2_ref[h] >= th_ref[h, i:i + 1, :]
            term = jnp.where(sel, b_ref[h], 0.0) * a_ref[h, i:i + 1, :]
            w = term if w is None else w + term
        z = act[i * PEER_N_KEYS:(i + 1) * PEER_N_KEYS, :]
        gelu = 0.5 * z * (1.0 + lax.erf(z * (1.0 / math.sqrt(2.0))))
        g_ref[i * PEER_N_KEYS:(i + 1) * PEER_N_KEYS, :] = (gelu * w).astype(BF16)
    acc_ref[...] += _dot(vt_ref[...], g_ref[...])

    @pl.when(e == pl.num_programs(1) - 1)
    def _():
        o_ref[...] = x1_ref[...] + mod_ref[0, 5:6, :] * acc_ref[...].T


def _pdense(h2t, u, vt, a, th, s2, b, x1, mod, seq, tn, n_i1):
    d, t = h2t.shape
    ne = u.shape[0]
    eb = n_i1 * PEER_N_KEYS
    nps = seq // tn
    gate = pl.BlockSpec((PEER_HEADS, n_i1, tn), lambda j, e: (0, e, j))
    full = pl.BlockSpec((PEER_HEADS, PEER_N_KEYS, tn), lambda j, e: (0, 0, j))
    return pl.pallas_call(
        functools.partial(_pdense_kernel, n_i1=n_i1),
        grid=(t // tn, ne // eb),
        in_specs=[pl.BlockSpec((d, tn), lambda j, e: (0, j)),
                  pl.BlockSpec((eb, d), lambda j, e: (e, 0)),
                  pl.BlockSpec((d, eb), lambda j, e: (0, e)),
                  gate, gate, full, full,
                  pl.BlockSpec((tn, d), lambda j, e: (j, 0)),
                  pl.BlockSpec((1, N_MOD_ROWS, d), lambda j, e: (j // nps, 0, 0))],
        out_specs=pl.BlockSpec((tn, d), lambda j, e: (j, 0)),
        out_shape=jax.ShapeDtypeStruct((t, d), F32),
        scratch_shapes=[pltpu.VMEM((d, tn), F32), pltpu.VMEM((eb, tn), BF16)],
        compiler_params=_params("arbitrary", "arbitrary"),
        name="pdense",
    )(h2t, u, vt, a, th, s2, b, x1, mod)


def _rot_matrix():
    n = QK_ROPE // 4
    r = np.zeros((QK_ROPE, QK_ROPE), np.float32)
    for axis in range(2):
        for i in range(n):
            r[axis * 2 * n + n + i, axis * 2 * n + i] = -1.0
            r[axis * 2 * n + i, axis * 2 * n + n + i] = 1.0
    return jnp.asarray(r)


def _rope_angles(seq):
    pos = jnp.arange(seq)
    row = (pos // GRID_W).astype(F32)
    col = (pos % GRID_W).astype(F32)
    n = QK_ROPE // 4
    inv = 1.0 / (ROPE_THETA ** (jnp.arange(n, dtype=F32) / n))
    ang_r = row[:, None] * inv
    ang_c = col[:, None] * inv
    return jnp.concatenate([ang_r, ang_r, ang_c, ang_c], axis=-1)


def _head_table(gain, cos, sin, scale):
    rows = cos.shape[0]
    nope = jnp.broadcast_to(gain[:QK_NOPE], (rows, QK_NOPE))
    return jnp.concatenate([nope, gain[QK_NOPE:] * cos, sin], axis=-1) * scale


def _slab_weights(w, gain, width):
    fan_in = w.shape[0]
    wh = w.reshape(fan_in, MLA_HEADS, width)
    nope, rope = wh[..., :QK_NOPE], wh[..., QK_NOPE:QK_HEAD]
    rot = jnp.einsum('khr,rs->khs', rope * gain[QK_NOPE:], _rot_matrix())
    return jnp.concatenate([nope, rope, rot], axis=-1).reshape(fan_in, MLA_HEADS * LANE)


def kernel(x, c, ctx, c_ctx, w_ada, b_ada, g_norm1, w_in, pool_w, pool_scale, g_q_lora, w_q_up, g_kv_lora,
           w_kv_up, g_qk_q, g_qk_k, w_out, g_norm2, peer_w_q, peer_sub_keys, peer_u, peer_v):
    bsz, seq, d = x.shape
    nctx = ctx.shape[1]
    t = bsz * seq
    o1, o2, o3 = POOL_WIDTH, POOL_WIDTH + Q_LORA, POOL_WIDTH + Q_LORA + KV_LORA

    rows = -(-(bsz + 1) // 16) * 16
    cvec = jnp.zeros((rows, d), F32).at[:bsz].set(c).at[bsz].set(c_ctx)
    mod_all = _ada(cvec, w_ada, b_ada).reshape(rows, 6, d)
    mod_all = jnp.pad(mod_all, ((0, 0), (0, N_MOD_ROWS - 6), (0, 0)))
    mod, modc = mod_all[:bsz], mod_all[bsz:bsz + 1]

    zeros = lambda r, cdim: jnp.zeros((r, cdim), F32)
    w_kr = w_in[:, o3:]
    w_krx = jnp.concatenate([zeros(d, QK_NOPE), w_kr, (w_kr * g_qk_k[QK_NOPE:]) @ _rot_matrix()], axis=-1)
    win = jnp.concatenate([w_in[:, :o3], w_krx], axis=-1).astype(BF16)
    win_ctx = jnp.concatenate([w_in[:, o2:o3], w_krx], axis=-1).astype(BF16)
    wq = _slab_weights(w_q_up, g_qk_q, QK_HEAD).astype(BF16)
    wkv_h = w_kv_up.reshape(KV_LORA, MLA_HEADS, QK_NOPE + V_HEAD)
    pad = jnp.zeros((KV_LORA, MLA_HEADS, LANE - QK_NOPE), F32)
    wk = jnp.concatenate([wkv_h[..., :QK_NOPE], pad], axis=-1).reshape(KV_LORA, MLA_HEADS * LANE)
    wv = jnp.concatenate([wkv_h[..., QK_NOPE:], pad], axis=-1).reshape(KV_LORA, MLA_HEADS * LANE)
    wkv = jnp.concatenate([wk, wv], axis=-1).astype(BF16)
    w_out_mla = w_out[POOL_WIDTH:].reshape(MLA_HEADS, V_HEAD, d)
    w_out_mla = jnp.pad(w_out_mla, ((0, 0), (0, LANE - V_HEAD), (0, 0))).reshape(MLA_HEADS * LANE, d).astype(BF16)
    w_out_pool = w_out[:POOL_WIDTH].astype(BF16)

    ang = _rope_angles(seq)
    tq = _head_table(g_qk_q, jnp.cos(ang), jnp.sin(ang), 1.0 / math.sqrt(QK_HEAD))
    tk = _head_table(g_qk_k, jnp.cos(ang), jnp.sin(ang), 1.0)
    tk_ctx = _head_table(g_qk_k, jnp.ones((8, QK_ROPE), F32), jnp.zeros((8, QK_ROPE), F32), 1.0)[:1]

    row = lambda v: v.reshape(1, -1)
    x2 = x.reshape(t, d)
    tm = min(512, seq)
    p_in, q, k_lat, v_lat = _proj(x2, mod, row(g_norm1), win, row(g_q_lora), wq, row(g_kv_lora), wkv,
                                  tq, tk, seq, tm)
    k_ctx, v_ctx = _proj_ctx(ctx.reshape(bsz * nctx, d), modc, row(g_norm1), win_ctx, row(g_kv_lora), wkv,
                             tk_ctx, nctx)
    pool_out = _pool(p_in.reshape(bsz, seq, POOL_WIDTH), pool_w, pool_scale)
    mla_out = _attn(q, k_lat, k_ctx, v_lat, v_ctx, bsz, seq, nctx, tm)
    x1, h2t = _mix(pool_out.reshape(t, POOL_WIDTH), mla_out, w_out_pool, w_out_mla, x2, mod, row(g_norm2),
                   seq, tm)

    a, th, s2, b = _pprep(h2t, peer_w_q.T.astype(BF16), peer_sub_keys.astype(BF16), 256)
    out = _pdense(h2t, peer_u.astype(BF16), peer_v.T.astype(BF16), a, th, s2, b, x1, mod, seq, tm, 8)
    return out.reshape(bsz, seq, d)
```

```python
import functools
import math

import jax
import jax.numpy as jnp
import numpy as np
from jax import lax
from jax.experimental import pallas as pl
from jax.experimental.pallas import tpu as pltpu

GRID_W = 64
POOL_WINDOWS = (2, 4, 8, 16)
POOL_GROUP = 128
POOL_WIDTH = POOL_GROUP * len(POOL_WINDOWS)
MLA_HEADS = 8
QK_NOPE = 64
QK_ROPE = 32
QK_HEAD = QK_NOPE + QK_ROPE
V_HEAD = 64
Q_LORA = 384
KV_LORA = 256
ROPE_THETA = 10000.0
PEER_HEADS = 8
PEER_N_KEYS = 128
PEER_HALF = 128
PEER_TOPK = 16
EPS = 1e-6

LANE = 128
N_MOD_ROWS = 8
N_RANK = PEER_TOPK + 1
VMEM_LIMIT = 48 * 1024 * 1024

BF16 = jnp.bfloat16
F32 = jnp.float32


def _params(*sem):
    return pltpu.CompilerParams(dimension_semantics=sem, vmem_limit_bytes=VMEM_LIMIT)


def _dot(a, b):
    return jnp.dot(a, b, preferred_element_type=F32)


def _split3(a):
    hi = a.astype(BF16)
    lo = (a - hi.astype(F32)).astype(BF16)
    return hi, lo


def _rms(xf, g):
    ms = jnp.mean(xf * xf, axis=-1, keepdims=True)
    return xf * lax.rsqrt(ms + EPS) * g


def _ada_kernel(c_ref, w_ref, b_ref, o_ref):
    c = c_ref[...]
    a = c / (1.0 + jnp.exp(-c))
    a_hi, a_lo = _split3(a)
    w_hi, w_lo = _split3(w_ref[...])
    o_ref[...] = _dot(a_hi, w_hi) + _dot(a_hi, w_lo) + _dot(a_lo, w_hi) + b_ref[...]


def _ada(cvec, w_ada, b_ada):
    rows, d = cvec.shape
    n = w_ada.shape[1]
    tn = 1536
    return pl.pallas_call(
        _ada_kernel,
        grid=(n // tn,),
        in_specs=[pl.BlockSpec((rows, d), lambda j: (0, 0)),
                  pl.BlockSpec((d, tn), lambda j: (0, j)),
                  pl.BlockSpec((1, tn), lambda j: (0, j))],
        out_specs=pl.BlockSpec((rows, tn), lambda j: (0, j)),
        out_shape=jax.ShapeDtypeStruct((rows, n), F32),
        compiler_params=_params("arbitrary"),
        name="ada",
    )(cvec, w_ada, b_ada.reshape(1, n))


def _head_norm_rope(slab, table):
    lane = lax.broadcasted_iota(jnp.int32, slab.shape, 1)
    ss = jnp.sum(jnp.where(lane < QK_HEAD, slab * slab, 0.0), axis=-1, keepdims=True)
    t = slab * lax.rsqrt(ss * (1.0 / QK_HEAD) + EPS) * table
    r = pltpu.roll(t, LANE - QK_ROPE, axis=1)
    return jnp.where(lane < QK_NOPE, t, jnp.where(lane < QK_HEAD, t + r, 0.0))


def _modulated(x_ref, mod_ref, gn_ref):
    h = _rms(x_ref[...], gn_ref[...])
    return (h * (1.0 + mod_ref[0, 1:2, :]) + mod_ref[0, 0:1, :]).astype(BF16)


def _keys_values(ckv, krx, gkv_ref, wkv_ref, tk, k_ref, v_ref):
    kvx = _dot(_rms(ckv, gkv_ref[...]).astype(BF16), wkv_ref[...])
    hw = MLA_HEADS * LANE
    for h in range(MLA_HEADS):
        slab = kvx[:, h * LANE:(h + 1) * LANE] + krx
        k_ref[:, h * LANE:(h + 1) * LANE] = _head_norm_rope(slab, tk).astype(BF16)
    v_ref[...] = kvx[:, hw:2 * hw].astype(BF16)


def _proj_kernel(x_ref, mod_ref, gn_ref, win_ref, gq_ref, wq_ref, gkv_ref, wkv_ref, tq_ref, tk_ref,
                 p_ref, q_ref, k_ref, v_ref):
    proj = _dot(_modulated(x_ref, mod_ref, gn_ref), win_ref[...])
    o1, o2, o3 = POOL_WIDTH, POOL_WIDTH + Q_LORA, POOL_WIDTH + Q_LORA + KV_LORA
    p_ref[...] = proj[:, :o1]
    qs = _dot(_rms(proj[:, o1:o2], gq_ref[...]).astype(BF16), wq_ref[...])
    tq = tq_ref[...]
    for h in range(MLA_HEADS):
        q_ref[:, h * LANE:(h + 1) * LANE] = _head_norm_rope(qs[:, h * LANE:(h + 1) * LANE], tq).astype(BF16)
    _keys_values(proj[:, o2:o3], proj[:, o3:o3 + LANE], gkv_ref, wkv_ref, tk_ref[...], k_ref, v_ref)


def _proj_ctx_kernel(x_ref, mod_ref, gn_ref, win_ref, gkv_ref, wkv_ref, tk_ref, k_ref, v_ref):
    proj = _dot(_modulated(x_ref, mod_ref, gn_ref), win_ref[...])
    _keys_values(proj[:, :KV_LORA], proj[:, KV_LORA:KV_LORA + LANE], gkv_ref, wkv_ref, tk_ref[...], k_ref, v_ref)


def _full(shape):
    return pl.BlockSpec(shape, lambda *_: (0,) * len(shape))


def _proj(x2, mod, gn, win, gq, wq, gkv, wkv, tq, tk, seq, tm):
    t, d = x2.shape
    nps = seq // tm
    hw = MLA_HEADS * LANE
    row = lambda w: pl.BlockSpec((tm, w), lambda i: (i, 0))
    tab = pl.BlockSpec((tm, LANE), lambda i: (i % nps, 0))
    return pl.pallas_call(
        _proj_kernel,
        grid=(t // tm,),
        in_specs=[row(d), pl.BlockSpec((1, N_MOD_ROWS, d), lambda i: (i // nps, 0, 0)), _full(gn.shape),
                  _full(win.shape), _full(gq.shape), _full(wq.shape), _full(gkv.shape), _full(wkv.shape),
                  tab, tab],
        out_specs=[row(POOL_WIDTH), row(hw), row(hw), row(hw)],
        out_shape=[jax.ShapeDtypeStruct((t, POOL_WIDTH), F32), jax.ShapeDtypeStruct((t, hw), BF16),
                   jax.ShapeDtypeStruct((t, hw), BF16), jax.ShapeDtypeStruct((t, hw), BF16)],
        compiler_params=_params("arbitrary"),
        name="proj",
    )(x2, mod, gn, win, gq, wq, gkv, wkv, tq, tk)


def _proj_ctx(c2, modc, gn, win, gkv, wkv, tk, tm):
    t, d = c2.shape
    hw = MLA_HEADS * LANE
    row = lambda w: pl.BlockSpec((tm, w), lambda i: (i, 0))
    return pl.pallas_call(
        _proj_ctx_kernel,
        grid=(t // tm,),
        in_specs=[row(d), _full(modc.shape), _full(gn.shape), _full(win.shape), _full(gkv.shape),
                  _full(wkv.shape), _full(tk.shape)],
        out_specs=[row(hw), row(hw)],
        out_shape=[jax.ShapeDtypeStruct((t, hw), BF16), jax.ShapeDtypeStruct((t, hw), BF16)],
        compiler_params=_params("arbitrary"),
        name="proj_ctx",
    )(c2, modc, gn, win, gkv, wkv, tk)


def _pool_kernel(p_ref, w_ref, sc_ref, o_ref):
    n = p_ref.shape[1]
    t = lax.broadcasted_iota(jnp.int32, (n, POOL_GROUP), 0)

    def shifted(a, d):
        r = pltpu.roll(a, (-d) % n, axis=0)
        return jnp.where((t + d >= 0) & (t + d < n), r, 0.0)

    for g, w in enumerate(POOL_WINDOWS):
        half = w // 2
        sl = slice(g * POOL_GROUP, (g + 1) * POOL_GROUP)
        pg = p_ref[0, :, sl]
        fwd, bwd, k = pg, shifted(pg, -1), 1
        while k < half:
            fwd = fwd + shifted(fwd, k)
            bwd = bwd + shifted(bwd, -k)
            k *= 2
        cnt = (jnp.minimum(t + half, n) - jnp.maximum(t - half, 0)).astype(F32)
        diff = (fwd + bwd) / cnt - pg
        y = _dot(diff.astype(BF16), w_ref[g].astype(BF16)) * sc_ref[:, sl]
        o_ref[0, :, sl] = y.astype(BF16)


def _pool(p3, pool_w, pool_scale):
    b, s, c = p3.shape
    return pl.pallas_call(
        _pool_kernel,
        grid=(b,),
        in_specs=[pl.BlockSpec((1, s, c), lambda i: (i, 0, 0)), _full(pool_w.shape), _full((1, c))],
        out_specs=pl.BlockSpec((1, s, c), lambda i: (i, 0, 0)),
        out_shape=jax.ShapeDtypeStruct((b, s, c), BF16),
        compiler_params=_params("arbitrary"),
        name="pool",
    )(p3, pool_w, pool_scale.reshape(1, c))


def _attn_kernel(q_ref, kl_ref, kc_ref, vl_ref, vc_ref, o_ref):
    nt = (((1,), (1,)), ((), ()))
    q = q_ref[...]
    s1 = lax.dot_general(q, kl_ref[...], nt, preferred_element_type=F32)
    s2 = lax.dot_general(q, kc_ref[...], nt, preferred_element_type=F32)
    m = jnp.maximum(jnp.max(s1, axis=-1, keepdims=True), jnp.max(s2, axis=-1, keepdims=True))
    p1 = jnp.exp(s1 - m)
    p2 = jnp.exp(s2 - m)
    den = jnp.sum(p1, axis=-1, keepdims=True) + jnp.sum(p2, axis=-1, keepdims=True)
    o = _dot(p1.astype(BF16), vl_ref[...]) + _dot(p2.astype(BF16), vc_ref[...])
    o_ref[...] = (o / den).astype(BF16)


def _attn(q, kl, kc, vl, vc, batch, seq, nctx, tq):
    t, hw = q.shape
    nq = seq // tq
    return pl.pallas_call(
        _attn_kernel,
        grid=(batch, MLA_HEADS, nq),
        in_specs=[pl.BlockSpec((tq, LANE), lambda b, h, i: (b * nq + i, h)),
                  pl.BlockSpec((seq, LANE), lambda b, h, i: (b, h)),
                  pl.BlockSpec((nctx, LANE), lambda b, h, i: (b, h)),
                  pl.BlockSpec((seq, LANE), lambda b, h, i: (b, h)),
                  pl.BlockSpec((nctx, LANE), lambda b, h, i: (b, h))],
        out_specs=pl.BlockSpec((tq, LANE), lambda b, h, i: (b * nq + i, h)),
        out_shape=jax.ShapeDtypeStruct((t, hw), BF16),
        compiler_params=_params("arbitrary", "arbitrary", "arbitrary"),
        name="attn",
    )(q, kl, kc, vl, vc)


def _mix_kernel(pool_ref, mla_ref, wp_ref, wm_ref, x_ref, mod_ref, gn_ref, x1_ref, h2t_ref):
    mix = _dot(pool_ref[...], wp_ref[...]) + _dot(mla_ref[...], wm_ref[...])
    x1 = x_ref[...] + mod_ref[0, 2:3, :] * mix
    x1_ref[...] = x1
    h2 = _rms(x1, gn_ref[...]) * (1.0 + mod_ref[0, 4:5, :]) + mod_ref[0, 3:4, :]
    h2t_ref[...] = h2.T.astype(BF16)


def _mix(pool2, mla2, wp, wm, x2, mod, gn, seq, tm):
    t, d = x2.shape
    nps = seq // tm
    row = lambda w: pl.BlockSpec((tm, w), lambda i: (i, 0))
    return pl.pallas_call(
        _mix_kernel,
        grid=(t // tm,),
        in_specs=[row(pool2.shape[1]), row(mla2.shape[1]), _full(wp.shape), _full(wm.shape), row(d),
                  pl.BlockSpec((1, N_MOD_ROWS, d), lambda i: (i // nps, 0, 0)), _full(gn.shape)],
        out_specs=[row(d), pl.BlockSpec((d, tm), lambda i: (0, i))],
        out_shape=[jax.ShapeDtypeStruct((t, d), F32), jax.ShapeDtypeStruct((d, t), BF16)],
        compiler_params=_params("arbitrary"),
        name="mix",
    )(pool2, mla2, wp, wm, x2, mod, gn)


def _staircase():
    return [(a, b) for a in range(N_RANK) for b in range(N_RANK) if (a + 1) * (b + 1) <= N_RANK]


def _pprep_kernel(h2t_ref, wq_ref, keys_ref, a_ref, cnt_ref, r2_ref, b_ref, qp_scr, top_scr, s_scr, r1_scr):
    qp_scr[...] = _dot(wq_ref[...], h2t_ref[...]).astype(BF16)
    neg = -jnp.inf
    for h in range(PEER_HEADS):
        for p in range(2):
            row0 = (h * 2 + p) * PEER_HALF
            s = _dot(keys_ref[p], qp_scr[row0:row0 + PEER_HALF, :])
            s_scr[p, h] = s
            rank = jnp.full(s.shape, float(N_RANK), F32)
            for r in range(N_RANK):
                m = jnp.max(s, axis=0, keepdims=True)
                top_scr[p, r, h:h + 1, :] = m
                hit = s == m
                rank = jnp.where(hit, float(r), rank)
                s = jnp.where(hit, neg, s)
            if p == 0:
                r1_scr[h] = rank
            else:
                r2_ref[h] = rank.astype(BF16)
    v1 = [top_scr[0, r] for r in range(N_RANK)]
    v2 = [top_scr[1, r] for r in range(N_RANK)]
    pairs = _staircase()
    cand = [v1[a] + v2[b] for a, b in pairs]
    work = list(cand)
    kth = None
    for r in range(N_RANK):
        prev = kth
        kth = functools.reduce(jnp.maximum, work)
        work = [jnp.where(c == kth, neg, c) for c in work]
    tau = 0.5 * (prev + kth)
    cmax = v1[0] + v2[0]
    keep = [c >= tau for c in cand]
    den = sum(jnp.where(k, jnp.exp(c - cmax), 0.0) for k, c in zip(keep, cand))
    inv = 1.0 / den
    width = [sum(jnp.where(k, 1.0, 0.0) for k, (a, _) in zip(keep, pairs) if a == r) for r in range(N_RANK)]
    for h in range(PEER_HEADS):
        hs = slice(h, h + 1)
        r1 = r1_scr[h]
        cnt = jnp.zeros(r1.shape, F32)
        for r in range(N_RANK):
            cnt = jnp.where(r1 == float(r), width[r][hs], cnt)
        cnt_ref[h] = cnt
        a_ref[h] = jnp.exp(s_scr[0, h] - v1[0][hs]) * inv[hs]
        b_ref[h] = jnp.exp(s_scr[1, h] - v2[0][hs]).astype(BF16)


def _pprep(h2t, wqt, keys, tn):
    d, t = h2t.shape
    blk = pl.BlockSpec((PEER_HEADS, PEER_N_KEYS, tn), lambda j: (0, 0, j))
    shp = lambda dt: jax.ShapeDtypeStruct((PEER_HEADS, PEER_N_KEYS, t), dt)
    return pl.pallas_call(
        _pprep_kernel,
        grid=(t // tn,),
        in_specs=[pl.BlockSpec((d, tn), lambda j: (0, j)), _full(wqt.shape), _full(keys.shape)],
        out_specs=[blk, blk, blk, blk],
        out_shape=[shp(F32), shp(F32), shp(BF16), shp(BF16)],
        scratch_shapes=[pltpu.VMEM((wqt.shape[0], tn), BF16),
                        pltpu.VMEM((2, N_RANK, PEER_HEADS, tn), F32),
                        pltpu.VMEM((2, PEER_HEADS, PEER_N_KEYS, tn), F32),
                        pltpu.VMEM((PEER_HEADS, PEER_N_KEYS, tn), F32)],
        compiler_params=_params("arbitrary"),
        name="pprep",
    )(h2t, wqt, keys)


def _pdense_kernel(h2t_ref, u_ref, vt_ref, a_ref, cnt_ref, r2_ref, b_ref, x1_ref, mod_ref, o_ref,
                   acc_ref, g_ref, *, n_i1):
    e = pl.program_id(1)

    @pl.when(e == 0)
    def _():
        acc_ref[...] = jnp.zeros_like(acc_ref)

    act = _dot(u_ref[...], h2t_ref[...])
    shape = (PEER_N_KEYS, act.shape[1])
    for i in range(n_i1):
        w = None
        for h in range(PEER_HEADS):
            cnt = jnp.broadcast_to(cnt_ref[h, i:i + 1, :], shape).astype(BF16)
            gate = jnp.broadcast_to(a_ref[h, i:i + 1, :], shape).astype(BF16)
            term = jnp.where(r2_ref[h] < cnt, b_ref[h], jnp.zeros((), BF16)) * gate
            w = term if w is None else w + term
        z = act[i * PEER_N_KEYS:(i + 1) * PEER_N_KEYS, :]
        gelu = 0.5 * z * (1.0 + lax.erf(z * (1.0 / math.sqrt(2.0))))
        g_ref[i * PEER_N_KEYS:(i + 1) * PEER_N_KEYS, :] = gelu.astype(BF16) * w
    acc_ref[...] += _dot(vt_ref[...], g_ref[...])

    @pl.when(e == pl.num_programs(1) - 1)
    def _():
        o_ref[...] = x1_ref[...] + mod_ref[0, 5:6, :] * acc_ref[...].T


def _pdense(h2t, u, vt, a, th, s2, b, x1, mod, seq, tn, n_i1):
    d, t = h2t.shape
    ne = u.shape[0]
    eb = n_i1 * PEER_N_KEYS
    nps = seq // tn
    gate = pl.BlockSpec((PEER_HEADS, n_i1, tn), lambda j, e: (0, e, j))
    full = pl.BlockSpec((PEER_HEADS, PEER_N_KEYS, tn), lambda j, e: (0, 0, j))
    return pl.pallas_call(
        functools.partial(_pdense_kernel, n_i1=n_i1),
        grid=(t // tn, ne // eb),
        in_specs=[pl.BlockSpec((d, tn), lambda j, e: (0, j)),
                  pl.BlockSpec((eb, d), lambda j, e: (e, 0)),
                  pl.BlockSpec((d, eb), lambda j, e: (0, e)),
                  gate, gate, full, full,
                  pl.BlockSpec((tn, d), lambda j, e: (j, 0)),
                  pl.BlockSpec((1, N_MOD_ROWS, d), lambda j, e: (j // nps, 0, 0))],
        out_specs=pl.BlockSpec((tn, d), lambda j, e: (j, 0)),
        out_shape=jax.ShapeDtypeStruct((t, d), F32),
        scratch_shapes=[pltpu.VMEM((d, tn), F32), pltpu.VMEM((eb, tn), BF16)],
        compiler_params=_params("arbitrary", "arbitrary"),
        name="pdense",
    )(h2t, u, vt, a, th, s2, b, x1, mod)


def _rot_matrix():
    n = QK_ROPE // 4
    r = np.zeros((QK_ROPE, QK_ROPE), np.float32)
    for axis in range(2):
        for i in range(n):
            r[axis * 2 * n + n + i, axis * 2 * n + i] = -1.0
            r[axis * 2 * n + i, axis * 2 * n + n + i] = 1.0
    return jnp.asarray(r)


def _rope_angles(seq):
    pos = jnp.arange(seq)
    row = (pos // GRID_W).astype(F32)
    col = (pos % GRID_W).astype(F32)
    n = QK_ROPE // 4
    inv = 1.0 / (ROPE_THETA ** (jnp.arange(n, dtype=F32) / n))
    ang_r = row[:, None] * inv
    ang_c = col[:, None] * inv
    return jnp.concatenate([ang_r, ang_r, ang_c, ang_c], axis=-1)


def _head_table(gain, cos, sin, scale):
    rows = cos.shape[0]
    nope = jnp.broadcast_to(gain[:QK_NOPE], (rows, QK_NOPE))
    return jnp.concatenate([nope, gain[QK_NOPE:] * cos, sin], axis=-1) * scale


def _slab_weights(w, gain, width):
    fan_in = w.shape[0]
    wh = w.reshape(fan_in, MLA_HEADS, width)
    nope, rope = wh[..., :QK_NOPE], wh[..., QK_NOPE:QK_HEAD]
    rot = jnp.einsum('khr,rs->khs', rope * gain[QK_NOPE:], _rot_matrix())
    return jnp.concatenate([nope, rope, rot], axis=-1).reshape(fan_in, MLA_HEADS * LANE)


def kernel(x, c, ctx, c_ctx, w_ada, b_ada, g_norm1, w_in, pool_w, pool_scale, g_q_lora, w_q_up, g_kv_lora,
           w_kv_up, g_qk_q, g_qk_k, w_out, g_norm2, peer_w_q, peer_sub_keys, peer_u, peer_v):
    bsz, seq, d = x.shape
    nctx = ctx.shape[1]
    t = bsz * seq
    o1, o2, o3 = POOL_WIDTH, POOL_WIDTH + Q_LORA, POOL_WIDTH + Q_LORA + KV_LORA

    rows = -(-(bsz + 1) // 16) * 16
    cvec = jnp.zeros((rows, d), F32).at[:bsz].set(c).at[bsz].set(c_ctx)
    mod_all = _ada(cvec, w_ada, b_ada).reshape(rows, 6, d)
    mod_all = jnp.pad(mod_all, ((0, 0), (0, N_MOD_ROWS - 6), (0, 0)))
    mod, modc = mod_all[:bsz], mod_all[bsz:bsz + 1]

    zeros = lambda r, cdim: jnp.zeros((r, cdim), F32)
    w_kr = w_in[:, o3:]
    w_krx = jnp.concatenate([zeros(d, QK_NOPE), w_kr, (w_kr * g_qk_k[QK_NOPE:]) @ _rot_matrix()], axis=-1)
    win = jnp.concatenate([w_in[:, :o3], w_krx], axis=-1).astype(BF16)
    win_ctx = jnp.concatenate([w_in[:, o2:o3], w_krx], axis=-1).astype(BF16)
    wq = _slab_weights(w_q_up, g_qk_q, QK_HEAD).astype(BF16)
    wkv_h = w_kv_up.reshape(KV_LORA, MLA_HEADS, QK_NOPE + V_HEAD)
    pad = jnp.zeros((KV_LORA, MLA_HEADS, LANE - QK_NOPE), F32)
    wk = jnp.concatenate([wkv_h[..., :QK_NOPE], pad], axis=-1).reshape(KV_LORA, MLA_HEADS * LANE)
    wv = jnp.concatenate([wkv_h[..., QK_NOPE:], pad], axis=-1).reshape(KV_LORA, MLA_HEADS * LANE)
    wkv = jnp.concatenate([wk, wv], axis=-1).astype(BF16)
    w_out_mla = w_out[POOL_WIDTH:].reshape(MLA_HEADS, V_HEAD, d)
    w_out_mla = jnp.pad(w_out_mla, ((0, 0), (0, LANE - V_HEAD), (0, 0))).reshape(MLA_HEADS * LANE, d).astype(BF16)
    w_out_pool = w_out[:POOL_WIDTH].astype(BF16)

    ang = _rope_angles(seq)
    tq = _head_table(g_qk_q, jnp.cos(ang), jnp.sin(ang), 1.0 / math.sqrt(QK_HEAD))
    tk = _head_table(g_qk_k, jnp.cos(ang), jnp.sin(ang), 1.0)
    tk_ctx = _head_table(g_qk_k, jnp.ones((8, QK_ROPE), F32), jnp.zeros((8, QK_ROPE), F32), 1.0)[:1]

    row = lambda v: v.reshape(1, -1)
    x2 = x.reshape(t, d)
    tm = min(512, seq)
    p_in, q, k_lat, v_lat = _proj(x2, mod, row(g_norm1), win, row(g_q_lora), wq, row(g_kv_lora), wkv,
                                  tq, tk, seq, tm)
    k_ctx, v_ctx = _proj_ctx(ctx.reshape(bsz * nctx, d), modc, row(g_norm1), win_ctx, row(g_kv_lora), wkv,
                             tk_ctx, nctx)
    pool_out = _pool(p_in.reshape(bsz, seq, POOL_WIDTH), pool_w, pool_scale)
    mla_out = _attn(q, k_lat, k_ctx, v_lat, v_ctx, bsz, seq, nctx, tm)
    x1, h2t = _mix(pool_out.reshape(t, POOL_WIDTH), mla_out, w_out_pool, w_out_mla, x2, mod, row(g_norm2),
                   seq, tm)

    a, th, s2, b = _pprep(h2t, peer_w_q.T.astype(BF16), peer_sub_keys.astype(BF16), 256)
    out = _pdense(h2t, peer_u.astype(BF16), peer_v.T.astype(BF16), a, th, s2, b, x1, mod, seq, tm, 8)
    return out.reshape(bsz, seq, d)
```

```python
import functools
import math

import jax
import jax.numpy as jnp
import numpy as np
from jax import lax
from jax.experimental import pallas as pl
from jax.experimental.pallas import tpu as pltpu

GRID_W = 64
POOL_WINDOWS = (2, 4, 8, 16)
POOL_GROUP = 128
POOL_WIDTH = POOL_GROUP * len(POOL_WINDOWS)
MLA_HEADS = 8
QK_NOPE = 64
QK_ROPE = 32
QK_HEAD = QK_NOPE + QK_ROPE
V_HEAD = 64
Q_LORA = 384
KV_LORA = 256
ROPE_THETA = 10000.0
PEER_HEADS = 8
PEER_N_KEYS = 128
PEER_HALF = 128
PEER_TOPK = 16
EPS = 1e-6

LANE = 128
N_MOD_ROWS = 8
N_RANK = PEER_TOPK + 1
VMEM_LIMIT = 48 * 1024 * 1024
MXU_ROWS_PER_DOT = 256
BF16 = jnp.bfloat16
F32 = jnp.float32


def _params(*sem):
    return pltpu.CompilerParams(dimension_semantics=sem, vmem_limit_bytes=VMEM_LIMIT)


def _dot(a, b):
    return jnp.dot(a, b, preferred_element_type=F32)


def _split3(a):
    hi = a.astype(BF16)
    lo = (a - hi.astype(F32)).astype(BF16)
    return hi, lo


def _rms(xf, g):
    ms = jnp.mean(xf * xf, axis=-1, keepdims=True)
    return xf * lax.rsqrt(ms + EPS) * g


def _ada_kernel(c_ref, w_ref, b_ref, o_ref):
    c = c_ref[...]
    a = c / (1.0 + jnp.exp(-c))
    a_hi, a_lo = _split3(a)
    w_hi, w_lo = _split3(w_ref[...])
    o_ref[...] = _dot(a_hi, w_hi) + _dot(a_hi, w_lo) + _dot(a_lo, w_hi) + b_ref[...]


def _ada(cvec, w_ada, b_ada):
    rows, d = cvec.shape
    n = w_ada.shape[1]
    tn = 1536
    return pl.pallas_call(
        _ada_kernel,
        grid=(n // tn,),
        in_specs=[pl.BlockSpec((rows, d), lambda j: (0, 0)),
                  pl.BlockSpec((d, tn), lambda j: (0, j)),
                  pl.BlockSpec((1, tn), lambda j: (0, j))],
        out_specs=pl.BlockSpec((rows, tn), lambda j: (0, j)),
        out_shape=jax.ShapeDtypeStruct((rows, n), F32),
        compiler_params=_params("arbitrary"),
        name="ada",
    )(cvec, w_ada, b_ada.reshape(1, n))


def _head_norm_rope(slab, table):
    lane = lax.broadcasted_iota(jnp.int32, slab.shape, 1)
    ss = jnp.sum(jnp.where(lane < QK_HEAD, slab * slab, 0.0), axis=-1, keepdims=True)
    t = slab * lax.rsqrt(ss * (1.0 / QK_HEAD) + EPS) * table
    r = pltpu.roll(t, LANE - QK_ROPE, axis=1)
    return jnp.where(lane < QK_NOPE, t, jnp.where(lane < QK_HEAD, t + r, 0.0))


def _modulated(x_ref, mod_ref, gn_ref):
    h = _rms(x_ref[...], gn_ref[...])
    return (h * (1.0 + mod_ref[0, 1:2, :]) + mod_ref[0, 0:1, :]).astype(BF16)


def _keys_values(ckv, krx, gkv_ref, wkv_ref, tk, k_ref, v_ref):
    kvx = _dot(_rms(ckv, gkv_ref[...]).astype(BF16), wkv_ref[...])
    hw = MLA_HEADS * LANE
    for h in range(MLA_HEADS):
        slab = kvx[:, h * LANE:(h + 1) * LANE] + krx
        k_ref[:, h * LANE:(h + 1) * LANE] = _head_norm_rope(slab, tk).astype(BF16)
    v_ref[...] = kvx[:, hw:2 * hw].astype(BF16)


def _proj_kernel(x_ref, mod_ref, gn_ref, win_ref, gq_ref, wq_ref, gkv_ref, wkv_ref, tq_ref, tk_ref,
                 p_ref, q_ref, k_ref, v_ref):
    proj = _dot(_modulated(x_ref, mod_ref, gn_ref), win_ref[...])
    o1, o2, o3 = POOL_WIDTH, POOL_WIDTH + Q_LORA, POOL_WIDTH + Q_LORA + KV_LORA
    p_ref[...] = proj[:, :o1]
    qs = _dot(_rms(proj[:, o1:o2], gq_ref[...]).astype(BF16), wq_ref[...])
    tq = tq_ref[...]
    for h in range(MLA_HEADS):
        q_ref[:, h * LANE:(h + 1) * LANE] = _head_norm_rope(qs[:, h * LANE:(h + 1) * LANE], tq).astype(BF16)
    _keys_values(proj[:, o2:o3], proj[:, o3:o3 + LANE], gkv_ref, wkv_ref, tk_ref[...], k_ref, v_ref)


def _proj_ctx_kernel(x_ref, mod_ref, gn_ref, win_ref, gkv_ref, wkv_ref, tk_ref, k_ref, v_ref):
    proj = _dot(_modulated(x_ref, mod_ref, gn_ref), win_ref[...])
    _keys_values(proj[:, :KV_LORA], proj[:, KV_LORA:KV_LORA + LANE], gkv_ref, wkv_ref, tk_ref[...], k_ref, v_ref)


def _full(shape):
    return pl.BlockSpec(shape, lambda *_: (0,) * len(shape))


def _proj(x2, mod, gn, win, gq, wq, gkv, wkv, tq, tk, seq, tm):
    t, d = x2.shape
    nps = seq // tm
    hw = MLA_HEADS * LANE
    row = lambda w: pl.BlockSpec((tm, w), lambda i: (i, 0))
    tab = pl.BlockSpec((tm, LANE), lambda i: (i % nps, 0))
    return pl.pallas_call(
        _proj_kernel,
        grid=(t // tm,),
        in_specs=[row(d), pl.BlockSpec((1, N_MOD_ROWS, d), lambda i: (i // nps, 0, 0)), _full(gn.shape),
                  _full(win.shape), _full(gq.shape), _full(wq.shape), _full(gkv.shape), _full(wkv.shape),
                  tab, tab],
        out_specs=[row(POOL_WIDTH), row(hw), row(hw), row(hw)],
        out_shape=[jax.ShapeDtypeStruct((t, POOL_WIDTH), F32), jax.ShapeDtypeStruct((t, hw), BF16),
                   jax.ShapeDtypeStruct((t, hw), BF16), jax.ShapeDtypeStruct((t, hw), BF16)],
        compiler_params=_params("arbitrary"),
        name="proj",
    )(x2, mod, gn, win, gq, wq, gkv, wkv, tq, tk)


def _proj_ctx(c2, modc, gn, win, gkv, wkv, tk, tm):
    t, d = c2.shape
    hw = MLA_HEADS * LANE
    row = lambda w: pl.BlockSpec((tm, w), lambda i: (i, 0))
    return pl.pallas_call(
        _proj_ctx_kernel,
        grid=(t // tm,),
        in_specs=[row(d), _full(modc.shape), _full(gn.shape), _full(win.shape), _full(gkv.shape),
                  _full(wkv.shape), _full(tk.shape)],
        out_specs=[row(hw), row(hw)],
        out_shape=[jax.ShapeDtypeStruct((t, hw), BF16), jax.ShapeDtypeStruct((t, hw), BF16)],
        compiler_params=_params("arbitrary"),
        name="proj_ctx",
    )(c2, modc, gn, win, gkv, wkv, tk)


def _pool_kernel(p_ref, w_ref, sc_ref, o_ref):
    n = p_ref.shape[1]
    t = lax.broadcasted_iota(jnp.int32, (n, POOL_GROUP), 0)

    def shifted(a, d):
        r = pltpu.roll(a, (-d) % n, axis=0)
        return jnp.where((t + d >= 0) & (t + d < n), r, 0.0)

    for g, w in enumerate(POOL_WINDOWS):
        half = w // 2
        sl = slice(g * POOL_GROUP, (g + 1) * POOL_GROUP)
        pg = p_ref[0, :, sl]
        fwd, bwd, k = pg, shifted(pg, -1), 1
        while k < half:
            fwd = fwd + shifted(fwd, k)
            bwd = bwd + shifted(bwd, -k)
            k *= 2
        cnt = (jnp.minimum(t + half, n) - jnp.maximum(t - half, 0)).astype(F32)
        diff = (fwd + bwd) / cnt - pg
        y = _dot(diff.astype(BF16), w_ref[g].astype(BF16)) * sc_ref[:, sl]
        o_ref[0, :, sl] = y.astype(BF16)


def _pool(p3, pool_w, pool_scale):
    b, s, c = p3.shape
    return pl.pallas_call(
        _pool_kernel,
        grid=(b,),
        in_specs=[pl.BlockSpec((1, s, c), lambda i: (i, 0, 0)), _full(pool_w.shape), _full((1, c))],
        out_specs=pl.BlockSpec((1, s, c), lambda i: (i, 0, 0)),
        out_shape=jax.ShapeDtypeStruct((b, s, c), BF16),
        compiler_params=_params("arbitrary"),
        name="pool",
    )(p3, pool_w, pool_scale.reshape(1, c))


def _attn_kernel(q_ref, kl_ref, kc_ref, vl_ref, vc_ref, o_ref):
    nt = (((1,), (1,)), ((), ()))
    q = q_ref[...]
    s1 = lax.dot_general(q, kl_ref[...], nt, preferred_element_type=F32)
    s2 = lax.dot_general(q, kc_ref[...], nt, preferred_element_type=F32)
    m = jnp.maximum(jnp.max(s1, axis=-1, keepdims=True), jnp.max(s2, axis=-1, keepdims=True))
    p1 = jnp.exp(s1 - m)
    p2 = jnp.exp(s2 - m)
    den = jnp.sum(p1, axis=-1, keepdims=True) + jnp.sum(p2, axis=-1, keepdims=True)
    o = _dot(p1.astype(BF16), vl_ref[...]) + _dot(p2.astype(BF16), vc_ref[...])
    o_ref[...] = (o / den).astype(BF16)


def _attn(q, kl, kc, vl, vc, batch, seq, nctx, tq):
    t, hw = q.shape
    nq = seq // tq
    return pl.pallas_call(
        _attn_kernel,
        grid=(batch, MLA_HEADS, nq),
        in_specs=[pl.BlockSpec((tq, LANE), lambda b, h, i: (b * nq + i, h)),
                  pl.BlockSpec((seq, LANE), lambda b, h, i: (b, h)),
                  pl.BlockSpec((nctx, LANE), lambda b, h, i: (b, h)),
                  pl.BlockSpec((seq, LANE), lambda b, h, i: (b, h)),
                  pl.BlockSpec((nctx, LANE), lambda b, h, i: (b, h))],
        out_specs=pl.BlockSpec((tq, LANE), lambda b, h, i: (b * nq + i, h)),
        out_shape=jax.ShapeDtypeStruct((t, hw), BF16),
        compiler_params=_params("arbitrary", "arbitrary", "arbitrary"),
        name="attn",
    )(q, kl, kc, vl, vc)


def _mix_kernel(pool_ref, mla_ref, wp_ref, wm_ref, x_ref, mod_ref, gn_ref, x1_ref, h2t_ref):
    mix = _dot(pool_ref[...], wp_ref[...]) + _dot(mla_ref[...], wm_ref[...])
    x1 = x_ref[...] + mod_ref[0, 2:3, :] * mix
    x1_ref[...] = x1
    h2 = _rms(x1, gn_ref[...]) * (1.0 + mod_ref[0, 4:5, :]) + mod_ref[0, 3:4, :]
    h2t_ref[...] = h2.T.astype(BF16)


def _mix(pool2, mla2, wp, wm, x2, mod, gn, seq, tm):
    t, d = x2.shape
    nps = seq // tm
    row = lambda w: pl.BlockSpec((tm, w), lambda i: (i, 0))
    return pl.pallas_call(
        _mix_kernel,
        grid=(t // tm,),
        in_specs=[row(pool2.shape[1]), row(mla2.shape[1]), _full(wp.shape), _full(wm.shape), row(d),
                  pl.BlockSpec((1, N_MOD_ROWS, d), lambda i: (i // nps, 0, 0)), _full(gn.shape)],
        out_specs=[row(d), pl.BlockSpec((d, tm), lambda i: (0, i))],
        out_shape=[jax.ShapeDtypeStruct((t, d), F32), jax.ShapeDtypeStruct((d, t), BF16)],
        compiler_params=_params("arbitrary"),
        name="mix",
    )(pool2, mla2, wp, wm, x2, mod, gn)


def _staircase():
    return [(a, b) for a in range(N_RANK) for b in range(N_RANK) if (a + 1) * (b + 1) <= N_RANK]


def _pprep_kernel(h2t_ref, wq_ref, keys_ref, a_ref, cnt_ref, r2_ref, b_ref, qp_scr, top_scr, s_scr, r1_scr):
    qp_scr[...] = _dot(wq_ref[...], h2t_ref[...]).astype(BF16)
    neg = -jnp.inf
    for h in range(PEER_HEADS):
        for p in range(2):
            row0 = (h * 2 + p) * PEER_HALF
            s = _dot(keys_ref[p], qp_scr[row0:row0 + PEER_HALF, :])
            s_scr[p, h] = s
            rank = jnp.full(s.shape, float(N_RANK), F32)
            for r in range(N_RANK):
                m = jnp.max(s, axis=0, keepdims=True)
                top_scr[p, r, h:h + 1, :] = m
                hit = s == m
                rank = jnp.where(hit, float(r), rank)
                s = jnp.where(hit, neg, s)
            if p == 0:
                r1_scr[h] = rank
            else:
                r2_ref[h] = rank.astype(BF16)
    v1 = [top_scr[0, r] for r in range(N_RANK)]
    v2 = [top_scr[1, r] for r in range(N_RANK)]
    pairs = _staircase()
    cand = [v1[a] + v2[b] for a, b in pairs]
    work = list(cand)
    kth = None
    for r in range(N_RANK):
        prev = kth
        kth = functools.reduce(jnp.maximum, work)
        work = [jnp.where(c == kth, neg, c) for c in work]
    tau = 0.5 * (prev + kth)
    cmax = v1[0] + v2[0]
    keep = [c >= tau for c in cand]
    den = sum(jnp.where(k, jnp.exp(c - cmax), 0.0) for k, c in zip(keep, cand))
    inv = 1.0 / den
    width = [sum(jnp.where(k, 1.0, 0.0) for k, (a, _) in zip(keep, pairs) if a == r) for r in range(N_RANK)]
    for h in range(PEER_HEADS):
        hs = slice(h, h + 1)
        r1 = r1_scr[h]
        cnt = jnp.zeros(r1.shape, F32)
        for r in range(N_RANK):
            cnt = jnp.where(r1 == float(r), width[r][hs], cnt)
        cnt_ref[h] = cnt
        a_ref[h] = jnp.exp(s_scr[0, h] - v1[0][hs]) * inv[hs]
        b_ref[h] = jnp.exp(s_scr[1, h] - v2[0][hs]).astype(BF16)


def _pprep(h2t, wqt, keys, tn):
    d, t = h2t.shape
    blk = pl.BlockSpec((PEER_HEADS, PEER_N_KEYS, tn), lambda j: (0, 0, j))
    shp = lambda dt: jax.ShapeDtypeStruct((PEER_HEADS, PEER_N_KEYS, t), dt)
    return pl.pallas_call(
        _pprep_kernel,
        grid=(t // tn,),
        in_specs=[pl.BlockSpec((d, tn), lambda j: (0, j)), _full(wqt.shape), _full(keys.shape)],
        out_specs=[blk, blk, blk, blk],
        out_shape=[shp(F32), shp(F32), shp(BF16), shp(BF16)],
        scratch_shapes=[pltpu.VMEM((wqt.shape[0], tn), BF16),
                        pltpu.VMEM((2, N_RANK, PEER_HEADS, tn), F32),
                        pltpu.VMEM((2, PEER_HEADS, PEER_N_KEYS, tn), F32),
                        pltpu.VMEM((PEER_HEADS, PEER_N_KEYS, tn), F32)],
        compiler_params=_params("arbitrary"),
        name="pprep",
    )(h2t, wqt, keys)


def _pdense_kernel(h2t_ref, u_ref, vt_ref, a_ref, cnt_ref, r2_ref, b_ref, x1_ref, mod_ref, o_ref,
                   acc_ref, g_ref, *, n_i1):
    e = pl.program_id(1)

    @pl.when(e == 0)
    def _():
        acc_ref[...] = jnp.zeros_like(acc_ref)

    shape = (PEER_N_KEYS, h2t_ref.shape[1])
    rows = MXU_ROWS_PER_DOT // PEER_N_KEYS
    for i in range(n_i1):
        if i % rows == 0:
            act = _dot(u_ref[i * PEER_N_KEYS:(i + rows) * PEER_N_KEYS, :], h2t_ref[...])
        w = None
        for h in range(PEER_HEADS):
            cnt = jnp.broadcast_to(cnt_ref[h, i:i + 1, :], shape).astype(BF16)
            gate = jnp.broadcast_to(a_ref[h, i:i + 1, :], shape).astype(BF16)
            term = jnp.where(r2_ref[h] < cnt, b_ref[h], jnp.zeros((), BF16)) * gate
            w = term if w is None else w + term
        z = act[(i % rows) * PEER_N_KEYS:(i % rows + 1) * PEER_N_KEYS, :]
        gelu = 0.5 * z * (1.0 + lax.erf(z * (1.0 / math.sqrt(2.0))))
        g_ref[i * PEER_N_KEYS:(i + 1) * PEER_N_KEYS, :] = gelu.astype(BF16) * w
    acc_ref[...] += _dot(vt_ref[...], g_ref[...])

    @pl.when(e == pl.num_programs(1) - 1)
    def _():
        o_ref[...] = x1_ref[...] + mod_ref[0, 5:6, :] * acc_ref[...].T


def _pdense(h2t, u, vt, a, cnt, r2, b, x1, mod, seq, tn, n_i1):
    d, t = h2t.shape
    ne = u.shape[0]
    eb = n_i1 * PEER_N_KEYS
    nps = seq // tn
    gate = pl.BlockSpec((PEER_HEADS, n_i1, tn), lambda j, e: (0, e, j))
    full = pl.BlockSpec((PEER_HEADS, PEER_N_KEYS, tn), lambda j, e: (0, 0, j))
    return pl.pallas_call(
        functools.partial(_pdense_kernel, n_i1=n_i1),
        grid=(t // tn, ne // eb),
        in_specs=[pl.BlockSpec((d, tn), lambda j, e: (0, j)),
                  pl.BlockSpec((eb, d), lambda j, e: (e, 0)),
                  pl.BlockSpec((d, eb), lambda j, e: (0, e)),
                  gate, gate, full, full,
                  pl.BlockSpec((tn, d), lambda j, e: (j, 0)),
                  pl.BlockSpec((1, N_MOD_ROWS, d), lambda j, e: (j // nps, 0, 0))],
        out_specs=pl.BlockSpec((tn, d), lambda j, e: (j, 0)),
        out_shape=jax.ShapeDtypeStruct((t, d), F32),
        scratch_shapes=[pltpu.VMEM((d, tn), F32), pltpu.VMEM((eb, tn), BF16)],
        compiler_params=_params("arbitrary", "arbitrary"),
        name="pdense",
    )(h2t, u, vt, a, cnt, r2, b, x1, mod)


def _rot_matrix():
    n = QK_ROPE // 4
    r = np.zeros((QK_ROPE, QK_ROPE), np.float32)
    for axis in range(2):
        for i in range(n):
            r[axis * 2 * n + n + i, axis * 2 * n + i] = -1.0
            r[axis * 2 * n + i, axis * 2 * n + n + i] = 1.0
    return jnp.asarray(r)


def _rope_angles(seq):
    pos = jnp.arange(seq)
    row = (pos // GRID_W).astype(F32)
    col = (pos % GRID_W).astype(F32)
    n = QK_ROPE // 4
    inv = 1.0 / (ROPE_THETA ** (jnp.arange(n, dtype=F32) / n))
    ang_r = row[:, None] * inv
    ang_c = col[:, None] * inv
    return jnp.concatenate([ang_r, ang_r, ang_c, ang_c], axis=-1)


def _head_table(gain, cos, sin, scale):
    rows = cos.shape[0]
    nope = jnp.broadcast_to(gain[:QK_NOPE], (rows, QK_NOPE))
    return jnp.concatenate([nope, gain[QK_NOPE:] * cos, sin], axis=-1) * scale


def _slab_weights(w, gain, width):
    fan_in = w.shape[0]
    wh = w.reshape(fan_in, MLA_HEADS, width)
    nope, rope = wh[..., :QK_NOPE], wh[..., QK_NOPE:QK_HEAD]
    rot = jnp.einsum('khr,rs->khs', rope * gain[QK_NOPE:], _rot_matrix())
    return jnp.concatenate([nope, rope, rot], axis=-1).reshape(fan_in, MLA_HEADS * LANE)


def kernel(x, c, ctx, c_ctx, w_ada, b_ada, g_norm1, w_in, pool_w, pool_scale, g_q_lora, w_q_up, g_kv_lora,
           w_kv_up, g_qk_q, g_qk_k, w_out, g_norm2, peer_w_q, peer_sub_keys, peer_u, peer_v):
    bsz, seq, d = x.shape
    nctx = ctx.shape[1]
    t = bsz * seq
    o1, o2, o3 = POOL_WIDTH, POOL_WIDTH + Q_LORA, POOL_WIDTH + Q_LORA + KV_LORA

    rows = -(-(bsz + 1) // 16) * 16
    cvec = jnp.zeros((rows, d), F32).at[:bsz].set(c).at[bsz].set(c_ctx)
    mod_all = _ada(cvec, w_ada, b_ada).reshape(rows, 6, d)
    mod_all = jnp.pad(mod_all, ((0, 0), (0, N_MOD_ROWS - 6), (0, 0)))
    mod, modc = mod_all[:bsz], mod_all[bsz:bsz + 1]

    zeros = lambda r, cdim: jnp.zeros((r, cdim), F32)
    w_kr = w_in[:, o3:]
    w_krx = jnp.concatenate([zeros(d, QK_NOPE), w_kr, (w_kr * g_qk_k[QK_NOPE:]) @ _rot_matrix()], axis=-1)
    win = jnp.concatenate([w_in[:, :o3], w_krx], axis=-1).astype(BF16)
    win_ctx = jnp.concatenate([w_in[:, o2:o3], w_krx], axis=-1).astype(BF16)
    wq = _slab_weights(w_q_up, g_qk_q, QK_HEAD).astype(BF16)
    wkv_h = w_kv_up.reshape(KV_LORA, MLA_HEADS, QK_NOPE + V_HEAD)
    pad = jnp.zeros((KV_LORA, MLA_HEADS, LANE - QK_NOPE), F32)
    wk = jnp.concatenate([wkv_h[..., :QK_NOPE], pad], axis=-1).reshape(KV_LORA, MLA_HEADS * LANE)
    wv = jnp.concatenate([wkv_h[..., QK_NOPE:], pad], axis=-1).reshape(KV_LORA, MLA_HEADS * LANE)
    wkv = jnp.concatenate([wk, wv], axis=-1).astype(BF16)
    w_out_mla = w_out[POOL_WIDTH:].reshape(MLA_HEADS, V_HEAD, d)
    w_out_mla = jnp.pad(w_out_mla, ((0, 0), (0, LANE - V_HEAD), (0, 0))).reshape(MLA_HEADS * LANE, d).astype(BF16)
    w_out_pool = w_out[:POOL_WIDTH].astype(BF16)

    ang = _rope_angles(seq)
    tq = _head_table(g_qk_q, jnp.cos(ang), jnp.sin(ang), 1.0 / math.sqrt(QK_HEAD))
    tk = _head_table(g_qk_k, jnp.cos(ang), jnp.sin(ang), 1.0)
    tk_ctx = _head_table(g_qk_k, jnp.ones((8, QK_ROPE), F32), jnp.zeros((8, QK_ROPE), F32), 1.0)[:1]

    row = lambda v: v.reshape(1, -1)
    x2 = x.reshape(t, d)
    tm = min(512, seq)
    p_in, q, k_lat, v_lat = _proj(x2, mod, row(g_norm1), win, row(g_q_lora), wq, row(g_kv_lora), wkv,
                                  tq, tk, seq, tm)
    k_ctx, v_ctx = _proj_ctx(ctx.reshape(bsz * nctx, d), modc, row(g_norm1), win_ctx, row(g_kv_lora), wkv,
                             tk_ctx, nctx)
    pool_out = _pool(p_in.reshape(bsz, seq, POOL_WIDTH), pool_w, pool_scale)
    mla_out = _attn(q, k_lat, k_ctx, v_lat, v_ctx, bsz, seq, nctx, tm)
    x1, h2t = _mix(pool_out.reshape(t, POOL_WIDTH), mla_out, w_out_pool, w_out_mla, x2, mod, row(g_norm2),
                   seq, tm)

    a, cnt, r2, b = _pprep(h2t, peer_w_q.T.astype(BF16), peer_sub_keys.astype(BF16), 256)
    out = _pdense(h2t, peer_u.astype(BF16), peer_v.T.astype(BF16), a, cnt, r2, b, x1, mod, seq, tm, 16)
    return out.reshape(bsz, seq, d)
```

```python
import functools
import math

import jax
import jax.numpy as jnp
import numpy as np
from jax import lax
from jax.experimental import pallas as pl
from jax.experimental.pallas import tpu as pltpu

GRID_W = 64
POOL_WINDOWS = (2, 4, 8, 16)
POOL_GROUP = 128
POOL_WIDTH = POOL_GROUP * len(POOL_WINDOWS)
MLA_HEADS = 8
QK_NOPE = 64
QK_ROPE = 32
QK_HEAD = QK_NOPE + QK_ROPE
V_HEAD = 64
Q_LORA = 384
KV_LORA = 256
ROPE_THETA = 10000.0
PEER_HEADS = 8
PEER_N_KEYS = 128
PEER_HALF = 128
PEER_TOPK = 16
EPS = 1e-6

LANE = 128
N_MOD_ROWS = 8
N_RANK = PEER_TOPK
VMEM_LIMIT = 48 * 1024 * 1024
MXU_ROWS_PER_DOT = 256
BF16 = jnp.bfloat16
F32 = jnp.float32


def _params(*sem):
    return pltpu.CompilerParams(dimension_semantics=sem, vmem_limit_bytes=VMEM_LIMIT)


def _dot(a, b):
    return jnp.dot(a, b, preferred_element_type=F32)


def _split3(a):
    hi = a.astype(BF16)
    lo = (a - hi.astype(F32)).astype(BF16)
    return hi, lo


def _rms(xf, g):
    ms = jnp.mean(xf * xf, axis=-1, keepdims=True)
    return xf * lax.rsqrt(ms + EPS) * g


def _ada_kernel(c_ref, w_ref, b_ref, o_ref):
    c = c_ref[...]
    a = c / (1.0 + jnp.exp(-c))
    a_hi, a_lo = _split3(a)
    w_hi, w_lo = _split3(w_ref[...])
    o_ref[...] = _dot(a_hi, w_hi) + _dot(a_hi, w_lo) + _dot(a_lo, w_hi) + b_ref[...]


def _ada(cvec, w_ada, b_ada):
    rows, d = cvec.shape
    n = w_ada.shape[1]
    tn = 1536
    return pl.pallas_call(
        _ada_kernel,
        grid=(n // tn,),
        in_specs=[pl.BlockSpec((rows, d), lambda j: (0, 0)),
                  pl.BlockSpec((d, tn), lambda j: (0, j)),
                  pl.BlockSpec((1, tn), lambda j: (0, j))],
        out_specs=pl.BlockSpec((rows, tn), lambda j: (0, j)),
        out_shape=jax.ShapeDtypeStruct((rows, n), F32),
        compiler_params=_params("arbitrary"),
        name="ada",
    )(cvec, w_ada, b_ada.reshape(1, n))


def _head_norm_rope(slab, table):
    lane = lax.broadcasted_iota(jnp.int32, slab.shape, 1)
    seg = (lax.broadcasted_iota(jnp.int32, (LANE, LANE), 0) < QK_HEAD).astype(BF16)
    ss = _dot((slab * slab).astype(BF16), seg)
    t = slab * lax.rsqrt(ss * (1.0 / QK_HEAD) + EPS) * table
    r = pltpu.roll(t, LANE - QK_ROPE, axis=1)
    return jnp.where(lane < QK_NOPE, t, jnp.where(lane < QK_HEAD, t + r, 0.0))


def _modulated(x, mod_ref, gn_ref):
    h = _rms(x, gn_ref[...])
    return (h * (1.0 + mod_ref[0, 1:2, :]) + mod_ref[0, 0:1, :]).astype(BF16)


def _row_chunks(ref):
    n = ref.shape[0]
    step = min(MXU_ROWS_PER_DOT, n)
    return [slice(r, r + step) for r in range(0, n, step)]


def _keys_values(ckv, krx, gkv_ref, wkv_ref, tk, k_ref, v_ref, rows):
    kvx = _dot(_rms(ckv, gkv_ref[...]).astype(BF16), wkv_ref[...])
    hw = MLA_HEADS * LANE
    for h in range(MLA_HEADS):
        slab = kvx[:, h * LANE:(h + 1) * LANE] + krx
        k_ref[rows, h * LANE:(h + 1) * LANE] = _head_norm_rope(slab, tk).astype(BF16)
    v_ref[rows, :] = kvx[:, hw:2 * hw].astype(BF16)


def _proj_kernel(x_ref, mod_ref, gn_ref, win_ref, gq_ref, wq_ref, gkv_ref, wkv_ref, tq_ref, tk_ref,
                 p_ref, q_ref, k_ref, v_ref):
    o1, o2, o3 = POOL_WIDTH, POOL_WIDTH + Q_LORA, POOL_WIDTH + Q_LORA + KV_LORA
    for rows in _row_chunks(x_ref):
        proj = _dot(_modulated(x_ref[rows, :], mod_ref, gn_ref), win_ref[...])
        p_ref[rows, :] = proj[:, :o1]
        qs = _dot(_rms(proj[:, o1:o2], gq_ref[...]).astype(BF16), wq_ref[...])
        tq = tq_ref[rows, :]
        for h in range(MLA_HEADS):
            q_ref[rows, h * LANE:(h + 1) * LANE] = _head_norm_rope(qs[:, h * LANE:(h + 1) * LANE], tq).astype(BF16)
        _keys_values(proj[:, o2:o3], proj[:, o3:o3 + LANE], gkv_ref, wkv_ref, tk_ref[rows, :], k_ref, v_ref, rows)


def _proj_ctx_kernel(x_ref, mod_ref, gn_ref, win_ref, gkv_ref, wkv_ref, tk_ref, k_ref, v_ref):
    for rows in _row_chunks(x_ref):
        proj = _dot(_modulated(x_ref[rows, :], mod_ref, gn_ref), win_ref[...])
        _keys_values(proj[:, :KV_LORA], proj[:, KV_LORA:KV_LORA + LANE], gkv_ref, wkv_ref, tk_ref[...], k_ref, v_ref,
                     rows)


def _full(shape):
    return pl.BlockSpec(shape, lambda *_: (0,) * len(shape))


def _proj(x2, mod, gn, win, gq, wq, gkv, wkv, tq, tk, seq, tm):
    t, d = x2.shape
    nps = seq // tm
    hw = MLA_HEADS * LANE
    row = lambda w: pl.BlockSpec((tm, w), lambda i: (i, 0))
    tab = pl.BlockSpec((tm, LANE), lambda i: (i % nps, 0))
    return pl.pallas_call(
        _proj_kernel,
        grid=(t // tm,),
        in_specs=[row(d), pl.BlockSpec((1, N_MOD_ROWS, d), lambda i: (i // nps, 0, 0)), _full(gn.shape),
                  _full(win.shape), _full(gq.shape), _full(wq.shape), _full(gkv.shape), _full(wkv.shape),
                  tab, tab],
        out_specs=[row(POOL_WIDTH), row(hw), row(hw), row(hw)],
        out_shape=[jax.ShapeDtypeStruct((t, POOL_WIDTH), F32), jax.ShapeDtypeStruct((t, hw), BF16),
                   jax.ShapeDtypeStruct((t, hw), BF16), jax.ShapeDtypeStruct((t, hw), BF16)],
        compiler_params=_params("arbitrary"),
        name="proj",
    )(x2, mod, gn, win, gq, wq, gkv, wkv, tq, tk)


def _proj_ctx(c2, modc, gn, win, gkv, wkv, tk, tm):
    t, d = c2.shape
    hw = MLA_HEADS * LANE
    row = lambda w: pl.BlockSpec((tm, w), lambda i: (i, 0))
    return pl.pallas_call(
        _proj_ctx_kernel,
        grid=(t // tm,),
        in_specs=[row(d), _full(modc.shape), _full(gn.shape), _full(win.shape), _full(gkv.shape),
                  _full(wkv.shape), _full(tk.shape)],
        out_specs=[row(hw), row(hw)],
        out_shape=[jax.ShapeDtypeStruct((t, hw), BF16), jax.ShapeDtypeStruct((t, hw), BF16)],
        compiler_params=_params("arbitrary"),
        name="proj_ctx",
    )(c2, modc, gn, win, gkv, wkv, tk)


def _pool_kernel(p_ref, w_ref, sc_ref, o_ref):
    n = p_ref.shape[1]
    t = lax.broadcasted_iota(jnp.int32, (n, POOL_GROUP), 0)

    def shifted(a, d):
        r = pltpu.roll(a, (-d) % n, axis=0)
        return jnp.where((t + d >= 0) & (t + d < n), r, 0.0)

    for g, w in enumerate(POOL_WINDOWS):
        half = w // 2
        sl = slice(g * POOL_GROUP, (g + 1) * POOL_GROUP)
        pg = p_ref[0, :, sl]
        fwd, bwd, k = pg, shifted(pg, -1), 1
        while k < half:
            fwd = fwd + shifted(fwd, k)
            bwd = bwd + shifted(bwd, -k)
            k *= 2
        cnt = (jnp.minimum(t + half, n) - jnp.maximum(t - half, 0)).astype(F32)
        diff = (fwd + bwd) / cnt - pg
        y = _dot(diff.astype(BF16), w_ref[g].astype(BF16)) * sc_ref[:, sl]
        o_ref[0, :, sl] = y.astype(BF16)


def _pool(p3, pool_w, pool_scale):
    b, s, c = p3.shape
    return pl.pallas_call(
        _pool_kernel,
        grid=(b,),
        in_specs=[pl.BlockSpec((1, s, c), lambda i: (i, 0, 0)), _full(pool_w.shape), _full((1, c))],
        out_specs=pl.BlockSpec((1, s, c), lambda i: (i, 0, 0)),
        out_shape=jax.ShapeDtypeStruct((b, s, c), BF16),
        compiler_params=_params("arbitrary"),
        name="pool",
    )(p3, pool_w, pool_scale.reshape(1, c))


def _attn_kernel(q_ref, kl_ref, kc_ref, vl_ref, vc_ref, o_ref):
    nt = (((1,), (1,)), ((), ()))
    for r in range(q_ref.shape[0] // MXU_ROWS_PER_DOT):
        rows = slice(r * MXU_ROWS_PER_DOT, (r + 1) * MXU_ROWS_PER_DOT)
        q = q_ref[rows, :]
        s1 = lax.dot_general(q, kl_ref[...], nt, preferred_element_type=F32)
        s2 = lax.dot_general(q, kc_ref[...], nt, preferred_element_type=F32)
        m = jnp.maximum(jnp.max(s1, axis=-1, keepdims=True), jnp.max(s2, axis=-1, keepdims=True))
        p1 = jnp.exp(s1 - m)
        p2 = jnp.exp(s2 - m)
        den = jnp.sum(p1, axis=-1, keepdims=True) + jnp.sum(p2, axis=-1, keepdims=True)
        o = _dot(p1.astype(BF16), vl_ref[...]) + _dot(p2.astype(BF16), vc_ref[...])
        o_ref[rows, :] = (o / den).astype(BF16)


def _attn(q, kl, kc, vl, vc, batch, seq, nctx, tq):
    t, hw = q.shape
    nq = seq // tq
    return pl.pallas_call(
        _attn_kernel,
        grid=(batch, MLA_HEADS, nq),
        in_specs=[pl.BlockSpec((tq, LANE), lambda b, h, i: (b * nq + i, h)),
                  pl.BlockSpec((seq, LANE), lambda b, h, i: (b, h)),
                  pl.BlockSpec((nctx, LANE), lambda b, h, i: (b, h)),
                  pl.BlockSpec((seq, LANE), lambda b, h, i: (b, h)),
                  pl.BlockSpec((nctx, LANE), lambda b, h, i: (b, h))],
        out_specs=pl.BlockSpec((tq, LANE), lambda b, h, i: (b * nq + i, h)),
        out_shape=jax.ShapeDtypeStruct((t, hw), BF16),
        compiler_params=_params("arbitrary", "arbitrary", "arbitrary"),
        name="attn",
    )(q, kl, kc, vl, vc)


def _mix_kernel(pool_ref, mla_ref, wp_ref, wm_ref, x_ref, mod_ref, gn_ref, x1_ref, h2t_ref):
    for rows in _row_chunks(x_ref):
        mix = _dot(pool_ref[rows, :], wp_ref[...]) + _dot(mla_ref[rows, :], wm_ref[...])
        x1 = x_ref[rows, :] + mod_ref[0, 2:3, :] * mix
        x1_ref[rows, :] = x1
        h2 = _rms(x1, gn_ref[...]) * (1.0 + mod_ref[0, 4:5, :]) + mod_ref[0, 3:4, :]
        h2t_ref[:, rows] = h2.T.astype(BF16)


def _mix(pool2, mla2, wp, wm, x2, mod, gn, seq, tm):
    t, d = x2.shape
    nps = seq // tm
    row = lambda w: pl.BlockSpec((tm, w), lambda i: (i, 0))
    return pl.pallas_call(
        _mix_kernel,
        grid=(t // tm,),
        in_specs=[row(pool2.shape[1]), row(mla2.shape[1]), _full(wp.shape), _full(wm.shape), row(d),
                  pl.BlockSpec((1, N_MOD_ROWS, d), lambda i: (i // nps, 0, 0)), _full(gn.shape)],
        out_specs=[row(d), pl.BlockSpec((d, tm), lambda i: (0, i))],
        out_shape=[jax.ShapeDtypeStruct((t, d), F32), jax.ShapeDtypeStruct((d, t), BF16)],
        compiler_params=_params("arbitrary"),
        name="mix",
    )(pool2, mla2, wp, wm, x2, mod, gn)


def _staircase():
    return [(a, b) for a in range(N_RANK) for b in range(N_RANK) if (a + 1) * (b + 1) <= N_RANK]


def _pprep_kernel(h2t_ref, wq_ref, keys_ref, a_ref, cnt_ref, r2_ref, b_ref, qp_scr, top_scr, s_scr):
    qp_scr[...] = _dot(wq_ref[...], h2t_ref[...]).astype(BF16)
    neg = -jnp.inf
    for h in range(PEER_HEADS):
        for p in range(2):
            row0 = (h * 2 + p) * PEER_HALF
            s = _dot(keys_ref[p], qp_scr[row0:row0 + PEER_HALF, :])
            s_scr[p, h] = s
            rank = jnp.full(s.shape, float(N_RANK), F32)
            for r in range(N_RANK):
                m = jnp.max(s, axis=0, keepdims=True)
                top_scr[p, r, h:h + 1, :] = m
                hit = s == m
                if p == 1:
                    rank = jnp.where(hit, float(r), rank)
                s = jnp.where(hit, neg, s)
            if p == 1:
                r2_ref[h] = rank.astype(BF16)
    v1 = [top_scr[0, r] for r in range(N_RANK)]
    v2 = [top_scr[1, r] for r in range(N_RANK)]
    pairs = _staircase()
    cand = [v1[a] + v2[b] for a, b in pairs]
    work = list(cand)
    for r in range(PEER_TOPK):
        kth = functools.reduce(jnp.maximum, work)
        work = [jnp.where(c == kth, neg, c) for c in work]
    cmax = v1[0] + v2[0]
    keep = [c >= kth for c in cand]
    den = sum(jnp.where(k, jnp.exp(c - cmax), 0.0) for k, c in zip(keep, cand))
    inv = 1.0 / den
    width = [sum(jnp.where(k, 1.0, 0.0) for k, (a, _) in zip(keep, pairs) if a == r) for r in range(N_RANK)]
    for h in range(PEER_HEADS):
        hs = slice(h, h + 1)
        s1 = s_scr[0, h]
        cnt = jnp.zeros(s1.shape, F32)
        for r in range(N_RANK):
            cnt = jnp.where(s1 == v1[r][hs], width[r][hs], cnt)
        cnt_ref[h] = cnt
        a_ref[h] = jnp.exp(s1 - v1[0][hs]) * inv[hs]
        b_ref[h] = jnp.exp(s_scr[1, h] - v2[0][hs]).astype(BF16)


def _pprep(h2t, wqt, keys, tn):
    d, t = h2t.shape
    blk = pl.BlockSpec((PEER_HEADS, PEER_N_KEYS, tn), lambda j: (0, 0, j))
    shp = lambda dt: jax.ShapeDtypeStruct((PEER_HEADS, PEER_N_KEYS, t), dt)
    return pl.pallas_call(
        _pprep_kernel,
        grid=(t // tn,),
        in_specs=[pl.BlockSpec((d, tn), lambda j: (0, j)), _full(wqt.shape), _full(keys.shape)],
        out_specs=[blk, blk, blk, blk],
        out_shape=[shp(F32), shp(F32), shp(BF16), shp(BF16)],
        scratch_shapes=[pltpu.VMEM((wqt.shape[0], tn), BF16),
                        pltpu.VMEM((2, N_RANK, PEER_HEADS, tn), F32),
                        pltpu.VMEM((2, PEER_HEADS, PEER_N_KEYS, tn), F32)],
        compiler_params=_params("arbitrary"),
        name="pprep",
    )(h2t, wqt, keys)


def _pdense_kernel(h2t_ref, u_ref, vt_ref, a_ref, cnt_ref, r2_ref, b_ref, x1_ref, mod_ref, o_ref,
                   acc_ref, g_ref, *, n_i1):
    e = pl.program_id(1)

    @pl.when(e == 0)
    def _():
        acc_ref[...] = jnp.zeros_like(acc_ref)

    shape = (PEER_N_KEYS, h2t_ref.shape[1])
    rows = MXU_ROWS_PER_DOT // PEER_N_KEYS
    for i in range(n_i1):
        if i % rows == 0:
            act = _dot(u_ref[i * PEER_N_KEYS:(i + rows) * PEER_N_KEYS, :], h2t_ref[...])
        w = None
        for h in range(PEER_HEADS):
            cnt = jnp.broadcast_to(cnt_ref[h, i:i + 1, :], shape).astype(BF16)
            gate = jnp.broadcast_to(a_ref[h, i:i + 1, :], shape).astype(BF16)
            term = jnp.where(r2_ref[h] < cnt, b_ref[h], jnp.zeros((), BF16)) * gate
            w = term if w is None else w + term
        z = act[(i % rows) * PEER_N_KEYS:(i % rows + 1) * PEER_N_KEYS, :]
        gelu = 0.5 * z * (1.0 + lax.erf(z * (1.0 / math.sqrt(2.0))))
        g_ref[i * PEER_N_KEYS:(i + 1) * PEER_N_KEYS, :] = gelu.astype(BF16) * w
    acc_ref[...] += _dot(vt_ref[...], g_ref[...])

    @pl.when(e == pl.num_programs(1) - 1)
    def _():
        o_ref[...] = x1_ref[...] + mod_ref[0, 5:6, :] * acc_ref[...].T


def _pdense(h2t, u, vt, a, cnt, r2, b, x1, mod, seq, tn, n_i1):
    d, t = h2t.shape
    ne = u.shape[0]
    eb = n_i1 * PEER_N_KEYS
    nps = seq // tn
    gate = pl.BlockSpec((PEER_HEADS, n_i1, tn), lambda j, e: (0, e, j))
    full = pl.BlockSpec((PEER_HEADS, PEER_N_KEYS, tn), lambda j, e: (0, 0, j))
    return pl.pallas_call(
        functools.partial(_pdense_kernel, n_i1=n_i1),
        grid=(t // tn, ne // eb),
        in_specs=[pl.BlockSpec((d, tn), lambda j, e: (0, j)),
                  pl.BlockSpec((eb, d), lambda j, e: (e, 0)),
                  pl.BlockSpec((d, eb), lambda j, e: (0, e)),
                  gate, gate, full, full,
                  pl.BlockSpec((tn, d), lambda j, e: (j, 0)),
                  pl.BlockSpec((1, N_MOD_ROWS, d), lambda j, e: (j // nps, 0, 0))],
        out_specs=pl.BlockSpec((tn, d), lambda j, e: (j, 0)),
        out_shape=jax.ShapeDtypeStruct((t, d), F32),
        scratch_shapes=[pltpu.VMEM((d, tn), F32), pltpu.VMEM((eb, tn), BF16)],
        compiler_params=_params("arbitrary", "arbitrary"),
        name="pdense",
    )(h2t, u, vt, a, cnt, r2, b, x1, mod)


def _rot_matrix():
    n = QK_ROPE // 4
    r = np.zeros((QK_ROPE, QK_ROPE), np.float32)
    for axis in range(2):
        for i in range(n):
            r[axis * 2 * n + n + i, axis * 2 * n + i] = -1.0
            r[axis * 2 * n + i, axis * 2 * n + n + i] = 1.0
    return jnp.asarray(r)


def _rope_angles(seq):
    pos = jnp.arange(seq)
    row = (pos // GRID_W).astype(F32)
    col = (pos % GRID_W).astype(F32)
    n = QK_ROPE // 4
    inv = 1.0 / (ROPE_THETA ** (jnp.arange(n, dtype=F32) / n))
    ang_r = row[:, None] * inv
    ang_c = col[:, None] * inv
    return jnp.concatenate([ang_r, ang_r, ang_c, ang_c], axis=-1)


def _head_table(gain, cos, sin, scale):
    rows = cos.shape[0]
    nope = jnp.broadcast_to(gain[:QK_NOPE], (rows, QK_NOPE))
    return jnp.concatenate([nope, gain[QK_NOPE:] * cos, sin], axis=-1) * scale


def _slab_weights(w, gain, width):
    fan_in = w.shape[0]
    wh = w.reshape(fan_in, MLA_HEADS, width)
    nope, rope = wh[..., :QK_NOPE], wh[..., QK_NOPE:QK_HEAD]
    rot = jnp.einsum('khr,rs->khs', rope * gain[QK_NOPE:], _rot_matrix())
    return jnp.concatenate([nope, rope, rot], axis=-1).reshape(fan_in, MLA_HEADS * LANE)


def kernel(x, c, ctx, c_ctx, w_ada, b_ada, g_norm1, w_in, pool_w, pool_scale, g_q_lora, w_q_up, g_kv_lora,
           w_kv_up, g_qk_q, g_qk_k, w_out, g_norm2, peer_w_q, peer_sub_keys, peer_u, peer_v):
    bsz, seq, d = x.shape
    nctx = ctx.shape[1]
    t = bsz * seq
    o1, o2, o3 = POOL_WIDTH, POOL_WIDTH + Q_LORA, POOL_WIDTH + Q_LORA + KV_LORA

    rows = -(-(bsz + 1) // 16) * 16
    cvec = jnp.zeros((rows, d), F32).at[:bsz].set(c).at[bsz].set(c_ctx)
    mod_all = _ada(cvec, w_ada, b_ada).reshape(rows, 6, d)
    mod_all = jnp.pad(mod_all, ((0, 0), (0, N_MOD_ROWS - 6), (0, 0)))
    mod, modc = mod_all[:bsz], mod_all[bsz:bsz + 1]

    zeros = lambda r, cdim: jnp.zeros((r, cdim), F32)
    w_kr = w_in[:, o3:]
    w_krx = jnp.concatenate([zeros(d, QK_NOPE), w_kr, (w_kr * g_qk_k[QK_NOPE:]) @ _rot_matrix()], axis=-1)
    win = jnp.concatenate([w_in[:, :o3], w_krx], axis=-1).astype(BF16)
    win_ctx = jnp.concatenate([w_in[:, o2:o3], w_krx], axis=-1).astype(BF16)
    wq = _slab_weights(w_q_up, g_qk_q, QK_HEAD).astype(BF16)
    wkv_h = w_kv_up.reshape(KV_LORA, MLA_HEADS, QK_NOPE + V_HEAD)
    pad = jnp.zeros((KV_LORA, MLA_HEADS, LANE - QK_NOPE), F32)
    wk = jnp.concatenate([wkv_h[..., :QK_NOPE], pad], axis=-1).reshape(KV_LORA, MLA_HEADS * LANE)
    wv = jnp.concatenate([wkv_h[..., QK_NOPE:], pad], axis=-1).reshape(KV_LORA, MLA_HEADS * LANE)
    wkv = jnp.concatenate([wk, wv], axis=-1).astype(BF16)
    w_out_mla = w_out[POOL_WIDTH:].reshape(MLA_HEADS, V_HEAD, d)
    w_out_mla = jnp.pad(w_out_mla, ((0, 0), (0, LANE - V_HEAD), (0, 0))).reshape(MLA_HEADS * LANE, d).astype(BF16)
    w_out_pool = w_out[:POOL_WIDTH].astype(BF16)

    ang = _rope_angles(seq)
    tq = _head_table(g_qk_q, jnp.cos(ang), jnp.sin(ang), 1.0 / math.sqrt(QK_HEAD))
    tk = _head_table(g_qk_k, jnp.cos(ang), jnp.sin(ang), 1.0)
    tk_ctx = _head_table(g_qk_k, jnp.ones((8, QK_ROPE), F32), jnp.zeros((8, QK_ROPE), F32), 1.0)[:1]

    row = lambda v: v.reshape(1, -1)
    x2 = x.reshape(t, d)
    tm = min(512, seq)
    p_in, q, k_lat, v_lat = _proj(x2, mod, row(g_norm1), win, row(g_q_lora), wq, row(g_kv_lora), wkv,
                                  tq, tk, seq, tm)
    k_ctx, v_ctx = _proj_ctx(ctx.reshape(bsz * nctx, d), modc, row(g_norm1), win_ctx, row(g_kv_lora), wkv,
                             tk_ctx, nctx)
    pool_out = _pool(p_in.reshape(bsz, seq, POOL_WIDTH), pool_w, pool_scale)
    mla_out = _attn(q, k_lat, k_ctx, v_lat, v_ctx, bsz, seq, nctx, min(1024, seq))
    x1, h2t = _mix(pool_out.reshape(t, POOL_WIDTH), mla_out, w_out_pool, w_out_mla, x2, mod, row(g_norm2),
                   seq, tm)

    a, cnt, r2, b = _pprep(h2t, peer_w_q.T.astype(BF16), peer_sub_keys.astype(BF16), 256)
    out = _pdense(h2t, peer_u.astype(BF16), peer_v.T.astype(BF16), a, cnt, r2, b, x1, mod, seq, tm, 16)
    return out.reshape(bsz, seq, d)
```

```python
import functools
import math

import jax
import jax.numpy as jnp
import numpy as np
from jax import lax
from jax.experimental import pallas as pl
from jax.experimental.pallas import tpu as pltpu

GRID_W = 64
POOL_WINDOWS = (2, 4, 8, 16)
POOL_GROUP = 128
POOL_WIDTH = POOL_GROUP * len(POOL_WINDOWS)
MLA_HEADS = 8
QK_NOPE = 64
QK_ROPE = 32
QK_HEAD = QK_NOPE + QK_ROPE
V_HEAD = 64
Q_LORA = 384
KV_LORA = 256
ROPE_THETA = 10000.0
PEER_HEADS = 8
PEER_N_KEYS = 128
PEER_HALF = 128
PEER_TOPK = 16
EPS = 1e-6

LANE = 128
SUBLANE = 8
N_MOD_ROWS = 8
N_RANK = PEER_TOPK
VMEM_LIMIT = 48 * 1024 * 1024
MXU_ROWS_PER_DOT = 256

BF16 = jnp.bfloat16
F32 = jnp.float32


def _params(*sem):
    return pltpu.CompilerParams(dimension_semantics=sem, vmem_limit_bytes=VMEM_LIMIT)


def _dot(a, b):
    return jnp.dot(a, b, preferred_element_type=F32)


def _split3(a):
    hi = a.astype(BF16)
    lo = (a - hi.astype(F32)).astype(BF16)
    return hi, lo


def _rms(xf, g):
    ms = jnp.mean(xf * xf, axis=-1, keepdims=True)
    return xf * lax.rsqrt(ms + EPS) * g


def _ada_kernel(c_ref, w_ref, b_ref, o_ref):
    c = c_ref[...]
    a = c / (1.0 + jnp.exp(-c))
    a_hi, a_lo = _split3(a)
    w_hi, w_lo = _split3(w_ref[...])
    o_ref[...] = _dot(a_hi, w_hi) + _dot(a_hi, w_lo) + _dot(a_lo, w_hi) + b_ref[...]


def _ada(cvec, w_ada, b_ada):
    rows, d = cvec.shape
    n = w_ada.shape[1]
    tn = 1536
    return pl.pallas_call(
        _ada_kernel,
        grid=(n // tn,),
        in_specs=[pl.BlockSpec((rows, d), lambda j: (0, 0)),
                  pl.BlockSpec((d, tn), lambda j: (0, j)),
                  pl.BlockSpec((1, tn), lambda j: (0, j))],
        out_specs=pl.BlockSpec((rows, tn), lambda j: (0, j)),
        out_shape=jax.ShapeDtypeStruct((rows, n), F32),
        compiler_params=_params("arbitrary"),
        name="ada",
    )(cvec, w_ada, b_ada.reshape(1, n))


def _head_norm_rope(slab, table):
    lane = lax.broadcasted_iota(jnp.int32, slab.shape, 1)
    seg = (lax.broadcasted_iota(jnp.int32, (LANE, LANE), 0) < QK_HEAD).astype(BF16)
    ss = _dot((slab * slab).astype(BF16), seg)
    t = slab * lax.rsqrt(ss * (1.0 / QK_HEAD) + EPS) * table
    r = pltpu.roll(t, LANE - QK_ROPE, axis=1)
    return jnp.where(lane < QK_NOPE, t, jnp.where(lane < QK_HEAD, t + r, 0.0))


def _modulated(x, mod_ref, gn_ref):
    h = _rms(x, gn_ref[...])
    return (h * (1.0 + mod_ref[0, 1:2, :]) + mod_ref[0, 0:1, :]).astype(BF16)


def _row_chunks(ref):
    n = ref.shape[0]
    step = min(MXU_ROWS_PER_DOT, n)
    return [slice(r, r + step) for r in range(0, n, step)]


def _keys_values(ckv, krx, gkv_ref, wkv_ref, tk, k_ref, v_ref, rows):
    kvx = _dot(_rms(ckv, gkv_ref[...]).astype(BF16), wkv_ref[...])
    hw = MLA_HEADS * LANE
    for h in range(MLA_HEADS):
        slab = kvx[:, h * LANE:(h + 1) * LANE] + krx
        k_ref[rows, h * LANE:(h + 1) * LANE] = _head_norm_rope(slab, tk).astype(BF16)
    v_ref[rows, :] = kvx[:, hw:2 * hw].astype(BF16)


def _proj_kernel(x_ref, mod_ref, gn_ref, win_ref, gq_ref, wq_ref, gkv_ref, wkv_ref, tq_ref, tk_ref,
                 p_ref, q_ref, k_ref, v_ref):
    o1, o2, o3 = POOL_WIDTH, POOL_WIDTH + Q_LORA, POOL_WIDTH + Q_LORA + KV_LORA
    for rows in _row_chunks(x_ref):
        proj = _dot(_modulated(x_ref[rows, :], mod_ref, gn_ref), win_ref[...])
        p_ref[rows, :] = proj[:, :o1]
        qs = _dot(_rms(proj[:, o1:o2], gq_ref[...]).astype(BF16), wq_ref[...])
        tq = tq_ref[rows, :]
        for h in range(MLA_HEADS):
            q_ref[rows, h * LANE:(h + 1) * LANE] = _head_norm_rope(qs[:, h * LANE:(h + 1) * LANE], tq).astype(BF16)
        _keys_values(proj[:, o2:o3], proj[:, o3:o3 + LANE], gkv_ref, wkv_ref, tk_ref[rows, :], k_ref, v_ref, rows)


def _proj_ctx_kernel(x_ref, mod_ref, gn_ref, win_ref, gkv_ref, wkv_ref, tk_ref, k_ref, v_ref):
    for rows in _row_chunks(x_ref):
        proj = _dot(_modulated(x_ref[rows, :], mod_ref, gn_ref), win_ref[...])
        _keys_values(proj[:, :KV_LORA], proj[:, KV_LORA:KV_LORA + LANE], gkv_ref, wkv_ref, tk_ref[...], k_ref, v_ref,
                     rows)


def _full(shape):
    return pl.BlockSpec(shape, lambda *_: (0,) * len(shape))


def _proj(x2, mod, gn, win, gq, wq, gkv, wkv, tq, tk, seq, tm):
    t, d = x2.shape
    nps = seq // tm
    hw = MLA_HEADS * LANE
    row = lambda w: pl.BlockSpec((tm, w), lambda i: (i, 0))
    tab = pl.BlockSpec((tm, LANE), lambda i: (i % nps, 0))
    return pl.pallas_call(
        _proj_kernel,
        grid=(t // tm,),
        in_specs=[row(d), pl.BlockSpec((1, N_MOD_ROWS, d), lambda i: (i // nps, 0, 0)), _full(gn.shape),
                  _full(win.shape), _full(gq.shape), _full(wq.shape), _full(gkv.shape), _full(wkv.shape),
                  tab, tab],
        out_specs=[row(POOL_WIDTH), row(hw), row(hw), row(hw)],
        out_shape=[jax.ShapeDtypeStruct((t, POOL_WIDTH), F32), jax.ShapeDtypeStruct((t, hw), BF16),
                   jax.ShapeDtypeStruct((t, hw), BF16), jax.ShapeDtypeStruct((t, hw), BF16)],
        compiler_params=_params("arbitrary"),
        name="proj",
    )(x2, mod, gn, win, gq, wq, gkv, wkv, tq, tk)


def _proj_ctx(c2, modc, gn, win, gkv, wkv, tk, tm):
    t, d = c2.shape
    hw = MLA_HEADS * LANE
    row = lambda w: pl.BlockSpec((tm, w), lambda i: (i, 0))
    return pl.pallas_call(
        _proj_ctx_kernel,
        grid=(t // tm,),
        in_specs=[row(d), _full(modc.shape), _full(gn.shape), _full(win.shape), _full(gkv.shape),
                  _full(wkv.shape), _full(tk.shape)],
        out_specs=[row(hw), row(hw)],
        out_shape=[jax.ShapeDtypeStruct((t, hw), BF16), jax.ShapeDtypeStruct((t, hw), BF16)],
        compiler_params=_params("arbitrary"),
        name="proj_ctx",
    )(c2, modc, gn, win, gkv, wkv, tk)


def _pool_kernel(p_ref, w_ref, sc_ref, o_ref):
    n = p_ref.shape[1]
    t = lax.broadcasted_iota(jnp.int32, (n, POOL_GROUP), 0)

    def shifted(a, d):
        r = pltpu.roll(a, (-d) % n, axis=0)
        return jnp.where((t + d >= 0) & (t + d < n), r, 0.0)

    for g, w in enumerate(POOL_WINDOWS):
        half = w // 2
        sl = slice(g * POOL_GROUP, (g + 1) * POOL_GROUP)
        pg = p_ref[0, :, sl]
        fwd, bwd, k = pg, shifted(pg, -1), 1
        while k < half:
            fwd = fwd + shifted(fwd, k)
            bwd = bwd + shifted(bwd, -k)
            k *= 2
        cnt = (jnp.minimum(t + half, n) - jnp.maximum(t - half, 0)).astype(F32)
        diff = (fwd + bwd) / cnt - pg
        y = _dot(diff.astype(BF16), w_ref[g].astype(BF16)) * sc_ref[:, sl]
        o_ref[0, :, sl] = y.astype(BF16)


def _pool(p3, pool_w, pool_scale):
    b, s, c = p3.shape
    return pl.pallas_call(
        _pool_kernel,
        grid=(b,),
        in_specs=[pl.BlockSpec((1, s, c), lambda i: (i, 0, 0)), _full(pool_w.shape), _full((1, c))],
        out_specs=pl.BlockSpec((1, s, c), lambda i: (i, 0, 0)),
        out_shape=jax.ShapeDtypeStruct((b, s, c), BF16),
        compiler_params=_params("arbitrary"),
        name="pool",
    )(p3, pool_w, pool_scale.reshape(1, c))


def _attn_kernel(q_ref, kl_ref, kc_ref, vl_ref, vc_ref, o_ref):
    nt = (((1,), (1,)), ((), ()))
    for r in range(q_ref.shape[0] // MXU_ROWS_PER_DOT):
        rows = slice(r * MXU_ROWS_PER_DOT, (r + 1) * MXU_ROWS_PER_DOT)
        q = q_ref[rows, :]
        s1 = lax.dot_general(q, kl_ref[...], nt, preferred_element_type=F32)
        s2 = lax.dot_general(q, kc_ref[...], nt, preferred_element_type=F32)
        m = jnp.maximum(jnp.max(s1, axis=-1, keepdims=True), jnp.max(s2, axis=-1, keepdims=True))
        p1 = jnp.exp(s1 - m)
        p2 = jnp.exp(s2 - m)
        den = jnp.sum(p1, axis=-1, keepdims=True) + jnp.sum(p2, axis=-1, keepdims=True)
        o = _dot(p1.astype(BF16), vl_ref[...]) + _dot(p2.astype(BF16), vc_ref[...])
        o_ref[rows, :] = (o / den).astype(BF16)


def _attn(q, kl, kc, vl, vc, batch, seq, nctx, tq):
    t, hw = q.shape
    nq = seq // tq
    return pl.pallas_call(
        _attn_kernel,
        grid=(batch, MLA_HEADS, nq),
        in_specs=[pl.BlockSpec((tq, LANE), lambda b, h, i: (b * nq + i, h)),
                  pl.BlockSpec((seq, LANE), lambda b, h, i: (b, h)),
                  pl.BlockSpec((nctx, LANE), lambda b, h, i: (b, h)),
                  pl.BlockSpec((seq, LANE), lambda b, h, i: (b, h)),
                  pl.BlockSpec((nctx, LANE), lambda b, h, i: (b, h))],
        out_specs=pl.BlockSpec((tq, LANE), lambda b, h, i: (b * nq + i, h)),
        out_shape=jax.ShapeDtypeStruct((t, hw), BF16),
        compiler_params=_params("arbitrary", "arbitrary", "arbitrary"),
        name="attn",
    )(q, kl, kc, vl, vc)


def _mix_kernel(pool_ref, mla_ref, wp_ref, wm_ref, x_ref, mod_ref, gn_ref, x1_ref, h2t_ref):
    for rows in _row_chunks(x_ref):
        mix = _dot(pool_ref[rows, :], wp_ref[...]) + _dot(mla_ref[rows, :], wm_ref[...])
        x1 = x_ref[rows, :] + mod_ref[0, 2:3, :] * mix
        x1_ref[rows, :] = x1
        h2 = _rms(x1, gn_ref[...]) * (1.0 + mod_ref[0, 4:5, :]) + mod_ref[0, 3:4, :]
        h2t_ref[:, rows] = h2.T.astype(BF16)


def _mix(pool2, mla2, wp, wm, x2, mod, gn, seq, tm):
    t, d = x2.shape
    nps = seq // tm
    row = lambda w: pl.BlockSpec((tm, w), lambda i: (i, 0))
    return pl.pallas_call(
        _mix_kernel,
        grid=(t // tm,),
        in_specs=[row(pool2.shape[1]), row(mla2.shape[1]), _full(wp.shape), _full(wm.shape), row(d),
                  pl.BlockSpec((1, N_MOD_ROWS, d), lambda i: (i // nps, 0, 0)), _full(gn.shape)],
        out_specs=[row(d), pl.BlockSpec((d, tm), lambda i: (0, i))],
        out_shape=[jax.ShapeDtypeStruct((t, d), F32), jax.ShapeDtypeStruct((d, t), BF16)],
        compiler_params=_params("arbitrary"),
        name="mix",
    )(pool2, mla2, wp, wm, x2, mod, gn)


def _staircase():
    return [(a, b) for a in range(N_RANK) for b in range(N_RANK) if (a + 1) * (b + 1) <= N_RANK]


def _sorting_network(n):
    def merge(lo, hi, r):
        step = r * 2
        if step < hi - lo:
            yield from merge(lo, hi, step)
            yield from merge(lo + r, hi, step)
            yield from ((i, i + r) for i in range(lo + r, hi - r, step))
        else:
            yield (lo, lo + r)

    def sort(lo, hi):
        if hi - lo >= 1:
            mid = lo + (hi - lo) // 2
            yield from sort(lo, mid)
            yield from sort(mid + 1, hi)
            yield from merge(lo, hi, 1)

    return list(sort(0, n - 1))


def _pprep_kernel(h2t_ref, wq_ref, keys_ref, a_ref, cnt_ref, r2_ref, b_ref, qp_scr, top_scr, s_scr):
    qp_scr[...] = _dot(wq_ref[...], h2t_ref[...]).astype(BF16)
    neg = -jnp.inf
    for h in range(PEER_HEADS):
        for p in range(2):
            row0 = (h * 2 + p) * PEER_HALF
            s = _dot(keys_ref[p], qp_scr[row0:row0 + PEER_HALF, :])
            s_scr[p, h] = s
            col = [s[g * SUBLANE:(g + 1) * SUBLANE, :] for g in range(PEER_N_KEYS // SUBLANE)]
            for i, j in _sorting_network(len(col)):
                col[i], col[j] = jnp.maximum(col[i], col[j]), jnp.minimum(col[i], col[j])
            for r in range(N_RANK):
                m = jnp.max(col[0], axis=0, keepdims=True)
                top_scr[p, r, h:h + 1, :] = m
                hit = col[0] == m
                for d in range(N_RANK - 1 - r):
                    col[d] = jnp.where(hit, col[d + 1], col[d])
    v1 = [top_scr[0, r] for r in range(N_RANK)]
    v2 = [top_scr[1, r] for r in range(N_RANK)]
    pairs = _staircase()
    cand = [v1[a] + v2[b] for a, b in pairs]
    work = list(cand)
    for r in range(PEER_TOPK):
        kth = functools.reduce(jnp.maximum, work)
        work = [jnp.where(c == kth, neg, c) for c in work]
    cmax = v1[0] + v2[0]
    keep = [c >= kth for c in cand]
    den = sum(jnp.where(k, jnp.exp(c - cmax), 0.0) for k, c in zip(keep, cand))
    inv = 0.5 / den
    width = [sum(jnp.where(k, 1.0, 0.0) for k, (a, _) in zip(keep, pairs) if a == r) for r in range(N_RANK)]
    for h in range(PEER_HEADS):
        hs = slice(h, h + 1)
        s1, s2 = s_scr[0, h], s_scr[1, h]
        cnt = jnp.zeros(s1.shape, F32)
        rank = jnp.full(s2.shape, float(N_RANK), F32)
        for r in range(N_RANK):
            cnt = jnp.where(s1 == v1[r][hs], width[r][hs], cnt)
            rank = jnp.where(s2 == v2[r][hs], float(r), rank)
        cnt_ref[h] = cnt
        r2_ref[h] = rank.astype(BF16)
        a_ref[h] = jnp.exp(s1 - v1[0][hs]) * inv[hs]
        b_ref[h] = jnp.exp(s2 - v2[0][hs]).astype(BF16)


def _pprep(h2t, wqt, keys, tn):
    d, t = h2t.shape
    blk = pl.BlockSpec((PEER_HEADS, PEER_N_KEYS, tn), lambda j: (0, 0, j))
    shp = lambda dt: jax.ShapeDtypeStruct((PEER_HEADS, PEER_N_KEYS, t), dt)
    return pl.pallas_call(
        _pprep_kernel,
        grid=(t // tn,),
        in_specs=[pl.BlockSpec((d, tn), lambda j: (0, j)), _full(wqt.shape), _full(keys.shape)],
        out_specs=[blk, blk, blk, blk],
        out_shape=[shp(F32), shp(F32), shp(BF16), shp(BF16)],
        scratch_shapes=[pltpu.VMEM((wqt.shape[0], tn), BF16),
                        pltpu.VMEM((2, N_RANK, PEER_HEADS, tn), F32),
                        pltpu.VMEM((2, PEER_HEADS, PEER_N_KEYS, tn), F32)],
        compiler_params=_params("arbitrary"),
        name="pprep",
    )(h2t, wqt, keys)


def _pdense_kernel(h2t_ref, u_ref, vt_ref, a_ref, cnt_ref, r2_ref, b_ref, x1_ref, mod_ref, o_ref,
                   acc_ref, g_ref, *, n_i1):
    e = pl.program_id(1)

    @pl.when(e == 0)
    def _():
        acc_ref[...] = jnp.zeros_like(acc_ref)

    shape = (PEER_N_KEYS, h2t_ref.shape[1])
    rows = MXU_ROWS_PER_DOT // PEER_N_KEYS
    for i in range(n_i1):
        if i % rows == 0:
            act = _dot(u_ref[i * PEER_N_KEYS:(i + rows) * PEER_N_KEYS, :], h2t_ref[...])
        w = None
        for h in range(PEER_HEADS):
            cnt = jnp.broadcast_to(cnt_ref[h, i:i + 1, :], shape).astype(BF16)
            gate = jnp.broadcast_to(a_ref[h, i:i + 1, :], shape).astype(BF16)
            term = jnp.where(r2_ref[h] < cnt, b_ref[h], jnp.zeros((), BF16)) * gate
            w = term if w is None else w + term
        z = act[(i % rows) * PEER_N_KEYS:(i % rows + 1) * PEER_N_KEYS, :]
        gelu = z * (1.0 + lax.erf(z * (1.0 / math.sqrt(2.0))))
        g_ref[i * PEER_N_KEYS:(i + 1) * PEER_N_KEYS, :] = gelu.astype(BF16) * w
    acc_ref[...] += _dot(vt_ref[...], g_ref[...])

    @pl.when(e == pl.num_programs(1) - 1)
    def _():
        o_ref[...] = x1_ref[...] + mod_ref[0, 5:6, :] * acc_ref[...].T


def _pdense(h2t, u, vt, a, cnt, r2, b, x1, mod, seq, tn, n_i1):
    d, t = h2t.shape
    ne = u.shape[0]
    eb = n_i1 * PEER_N_KEYS
    nps = seq // tn
    gate = pl.BlockSpec((PEER_HEADS, n_i1, tn), lambda j, e: (0, e, j))
    full = pl.BlockSpec((PEER_HEADS, PEER_N_KEYS, tn), lambda j, e: (0, 0, j))
    return pl.pallas_call(
        functools.partial(_pdense_kernel, n_i1=n_i1),
        grid=(t // tn, ne // eb),
        in_specs=[pl.BlockSpec((d, tn), lambda j, e: (0, j)),
                  pl.BlockSpec((eb, d), lambda j, e: (e, 0)),
                  pl.BlockSpec((d, eb), lambda j, e: (0, e)),
                  gate, gate, full, full,
                  pl.BlockSpec((tn, d), lambda j, e: (j, 0)),
                  pl.BlockSpec((1, N_MOD_ROWS, d), lambda j, e: (j // nps, 0, 0))],
        out_specs=pl.BlockSpec((tn, d), lambda j, e: (j, 0)),
        out_shape=jax.ShapeDtypeStruct((t, d), F32),
        scratch_shapes=[pltpu.VMEM((d, tn), F32), pltpu.VMEM((eb, tn), BF16)],
        compiler_params=_params("arbitrary", "arbitrary"),
        name="pdense",
    )(h2t, u, vt, a, cnt, r2, b, x1, mod)


def _rot_matrix():
    n = QK_ROPE // 4
    r = np.zeros((QK_ROPE, QK_ROPE), np.float32)
    for axis in range(2):
        for i in range(n):
            r[axis * 2 * n + n + i, axis * 2 * n + i] = -1.0
            r[axis * 2 * n + i, axis * 2 * n + n + i] = 1.0
    return jnp.asarray(r)


def _rope_angles(seq):
    pos = jnp.arange(seq)
    row = (pos // GRID_W).astype(F32)
    col = (pos % GRID_W).astype(F32)
    n = QK_ROPE // 4
    inv = 1.0 / (ROPE_THETA ** (jnp.arange(n, dtype=F32) / n))
    ang_r = row[:, None] * inv
    ang_c = col[:, None] * inv
    return jnp.concatenate([ang_r, ang_r, ang_c, ang_c], axis=-1)


def _head_table(gain, cos, sin, scale):
    rows = cos.shape[0]
    nope = jnp.broadcast_to(gain[:QK_NOPE], (rows, QK_NOPE))
    return jnp.concatenate([nope, gain[QK_NOPE:] * cos, sin], axis=-1) * scale


def _slab_weights(w, gain, width):
    fan_in = w.shape[0]
    wh = w.reshape(fan_in, MLA_HEADS, width)
    nope, rope = wh[..., :QK_NOPE], wh[..., QK_NOPE:QK_HEAD]
    rot = jnp.einsum('khr,rs->khs', rope * gain[QK_NOPE:], _rot_matrix())
    return jnp.concatenate([nope, rope, rot], axis=-1).reshape(fan_in, MLA_HEADS * LANE)


def kernel(x, c, ctx, c_ctx, w_ada, b_ada, g_norm1, w_in, pool_w, pool_scale, g_q_lora, w_q_up, g_kv_lora,
           w_kv_up, g_qk_q, g_qk_k, w_out, g_norm2, peer_w_q, peer_sub_keys, peer_u, peer_v):
    bsz, seq, d = x.shape
    nctx = ctx.shape[1]
    t = bsz * seq
    o1, o2, o3 = POOL_WIDTH, POOL_WIDTH + Q_LORA, POOL_WIDTH + Q_LORA + KV_LORA

    rows = -(-(bsz + 1) // 16) * 16
    cvec = jnp.zeros((rows, d), F32).at[:bsz].set(c).at[bsz].set(c_ctx)
    mod_all = _ada(cvec, w_ada, b_ada).reshape(rows, 6, d)
    mod_all = jnp.pad(mod_all, ((0, 0), (0, N_MOD_ROWS - 6), (0, 0)))
    mod, modc = mod_all[:bsz], mod_all[bsz:bsz + 1]

    zeros = lambda r, cdim: jnp.zeros((r, cdim), F32)
    w_kr = w_in[:, o3:]
    w_krx = jnp.concatenate([zeros(d, QK_NOPE), w_kr, (w_kr * g_qk_k[QK_NOPE:]) @ _rot_matrix()], axis=-1)
    win = jnp.concatenate([w_in[:, :o3], w_krx], axis=-1).astype(BF16)
    win_ctx = jnp.concatenate([w_in[:, o2:o3], w_krx], axis=-1).astype(BF16)
    wq = _slab_weights(w_q_up, g_qk_q, QK_HEAD).astype(BF16)
    wkv_h = w_kv_up.reshape(KV_LORA, MLA_HEADS, QK_NOPE + V_HEAD)
    pad = jnp.zeros((KV_LORA, MLA_HEADS, LANE - QK_NOPE), F32)
    wk = jnp.concatenate([wkv_h[..., :QK_NOPE], pad], axis=-1).reshape(KV_LORA, MLA_HEADS * LANE)
    wv = jnp.concatenate([wkv_h[..., QK_NOPE:], pad], axis=-1).reshape(KV_LORA, MLA_HEADS * LANE)
    wkv = jnp.concatenate([wk, wv], axis=-1).astype(BF16)
    w_out_mla = w_out[POOL_WIDTH:].reshape(MLA_HEADS, V_HEAD, d)
    w_out_mla = jnp.pad(w_out_mla, ((0, 0), (0, LANE - V_HEAD), (0, 0))).reshape(MLA_HEADS * LANE, d).astype(BF16)
    w_out_pool = w_out[:POOL_WIDTH].astype(BF16)

    ang = _rope_angles(seq)
    tq = _head_table(g_qk_q, jnp.cos(ang), jnp.sin(ang), 1.0 / math.sqrt(QK_HEAD))
    tk = _head_table(g_qk_k, jnp.cos(ang), jnp.sin(ang), 1.0)
    tk_ctx = _head_table(g_qk_k, jnp.ones((8, QK_ROPE), F32), jnp.zeros((8, QK_ROPE), F32), 1.0)[:1]

    row = lambda v: v.reshape(1, -1)
    x2 = x.reshape(t, d)
    tm = min(512, seq)
    p_in, q, k_lat, v_lat = _proj(x2, mod, row(g_norm1), win, row(g_q_lora), wq, row(g_kv_lora), wkv,
                                  tq, tk, seq, tm)
    k_ctx, v_ctx = _proj_ctx(ctx.reshape(bsz * nctx, d), modc, row(g_norm1), win_ctx, row(g_kv_lora), wkv,
                             tk_ctx, nctx)
    pool_out = _pool(p_in.reshape(bsz, seq, POOL_WIDTH), pool_w, pool_scale)
    mla_out = _attn(q, k_lat, k_ctx, v_lat, v_ctx, bsz, seq, nctx, min(1024, seq))
    x1, h2t = _mix(pool_out.reshape(t, POOL_WIDTH), mla_out, w_out_pool, w_out_mla, x2, mod, row(g_norm2),
                   seq, tm)

    a, cnt, r2, b = _pprep(h2t, peer_w_q.T.astype(BF16), peer_sub_keys.astype(BF16), 256)
    out = _pdense(h2t, peer_u.astype(BF16), peer_v.T.astype(BF16), a, cnt, r2, b, x1, mod, seq, tm, 16)
    return out.reshape(bsz, seq, d)
```

```python
import functools
import math

import jax
import jax.numpy as jnp
import numpy as np
from jax import lax
from jax.experimental import pallas as pl
from jax.experimental.pallas import tpu as pltpu

GRID_W = 64
POOL_WINDOWS = (2, 4, 8, 16)
POOL_GROUP = 128
POOL_WIDTH = POOL_GROUP * len(POOL_WINDOWS)
MLA_HEADS = 8
QK_NOPE = 64
QK_ROPE = 32
QK_HEAD = QK_NOPE + QK_ROPE
V_HEAD = 64
Q_LORA = 384
KV_LORA = 256
ROPE_THETA = 10000.0
PEER_HEADS = 8
PEER_N_KEYS = 128
PEER_HALF = 128
PEER_TOPK = 16
EPS = 1e-6

LANE = 128
SUBLANE = 8
N_MOD_ROWS = 8
N_RANK = PEER_TOPK
VMEM_LIMIT = 48 * 1024 * 1024
PDENSE_VMEM_LIMIT = 58 * 1024 * 1024
MXU_ROWS_PER_DOT = 256
TOKEN_TILE = 512
PPREP_TOKEN_TILE = 256
PDENSE_KEYS_PER_STEP = 32

BF16 = jnp.bfloat16
F32 = jnp.float32


def _params(*sem, vmem=None):
    return pltpu.CompilerParams(dimension_semantics=sem, vmem_limit_bytes=vmem or VMEM_LIMIT)


def _dot(a, b):
    return jnp.dot(a, b, preferred_element_type=F32)


def _split3(a):
    hi = a.astype(BF16)
    lo = (a - hi.astype(F32)).astype(BF16)
    return hi, lo


def _rms(xf, g):
    ms = jnp.mean(xf * xf, axis=-1, keepdims=True)
    return xf * lax.rsqrt(ms + EPS) * g


def _ada_kernel(c_ref, w_ref, b_ref, o_ref):
    c = c_ref[...]
    a = c / (1.0 + jnp.exp(-c))
    a_hi, a_lo = _split3(a)
    w_hi, w_lo = _split3(w_ref[...])
    o_ref[...] = _dot(a_hi, w_hi) + _dot(a_hi, w_lo) + _dot(a_lo, w_hi) + b_ref[...]


def _ada(cvec, w_ada, b_ada):
    rows, d = cvec.shape
    n = w_ada.shape[1]
    tn = 1536
    return pl.pallas_call(
        _ada_kernel,
        grid=(n // tn,),
        in_specs=[pl.BlockSpec((rows, d), lambda j: (0, 0)),
                  pl.BlockSpec((d, tn), lambda j: (0, j)),
                  pl.BlockSpec((1, tn), lambda j: (0, j))],
        out_specs=pl.BlockSpec((rows, tn), lambda j: (0, j)),
        out_shape=jax.ShapeDtypeStruct((rows, n), F32),
        compiler_params=_params("arbitrary"),
        name="ada",
    )(cvec, w_ada, b_ada.reshape(1, n))


def _head_norm_rope(slab, table):
    lane = lax.broadcasted_iota(jnp.int32, slab.shape, 1)
    seg = (lax.broadcasted_iota(jnp.int32, (LANE, LANE), 0) < QK_HEAD).astype(BF16)
    ss = _dot((slab * slab).astype(BF16), seg)
    t = slab * lax.rsqrt(ss * (1.0 / QK_HEAD) + EPS) * table
    r = pltpu.roll(t, LANE - QK_ROPE, axis=1)
    return jnp.where(lane < QK_NOPE, t, jnp.where(lane < QK_HEAD, t + r, 0.0))


def _modulated(x, mod_ref, gn_ref):
    h = _rms(x, gn_ref[...])
    return (h * (1.0 + mod_ref[0, 1:2, :]) + mod_ref[0, 0:1, :]).astype(BF16)


def _row_chunks(ref):
    n = ref.shape[0]
    step = min(MXU_ROWS_PER_DOT, n)
    return [slice(r, r + step) for r in range(0, n, step)]


def _keys_values(ckv, krx, gkv_ref, wkv_ref, tk, k_ref, v_ref, rows):
    kvx = _dot(_rms(ckv, gkv_ref[...]).astype(BF16), wkv_ref[...])
    hw = MLA_HEADS * LANE
    for h in range(MLA_HEADS):
        slab = kvx[:, h * LANE:(h + 1) * LANE] + krx
        k_ref[rows, h * LANE:(h + 1) * LANE] = _head_norm_rope(slab, tk).astype(BF16)
    v_ref[rows, :] = kvx[:, hw:2 * hw].astype(BF16)


def _proj_kernel(x_ref, mod_ref, gn_ref, win_ref, gq_ref, wq_ref, gkv_ref, wkv_ref, tq_ref, tk_ref,
                 p_ref, q_ref, k_ref, v_ref):
    o1, o2, o3 = POOL_WIDTH, POOL_WIDTH + Q_LORA, POOL_WIDTH + Q_LORA + KV_LORA
    for rows in _row_chunks(x_ref):
        proj = _dot(_modulated(x_ref[rows, :], mod_ref, gn_ref), win_ref[...])
        p_ref[rows, :] = proj[:, :o1]
        qs = _dot(_rms(proj[:, o1:o2], gq_ref[...]).astype(BF16), wq_ref[...])
        tq = tq_ref[rows, :]
        for h in range(MLA_HEADS):
            q_ref[rows, h * LANE:(h + 1) * LANE] = _head_norm_rope(qs[:, h * LANE:(h + 1) * LANE], tq).astype(BF16)
        _keys_values(proj[:, o2:o3], proj[:, o3:o3 + LANE], gkv_ref, wkv_ref, tk_ref[rows, :], k_ref, v_ref, rows)


def _proj_ctx_kernel(x_ref, mod_ref, gn_ref, win_ref, gkv_ref, wkv_ref, tk_ref, k_ref, v_ref):
    for rows in _row_chunks(x_ref):
        proj = _dot(_modulated(x_ref[rows, :], mod_ref, gn_ref), win_ref[...])
        _keys_values(proj[:, :KV_LORA], proj[:, KV_LORA:KV_LORA + LANE], gkv_ref, wkv_ref, tk_ref[...], k_ref, v_ref,
                     rows)


def _full(shape):
    return pl.BlockSpec(shape, lambda *_: (0,) * len(shape))


def _proj(x2, mod, gn, win, gq, wq, gkv, wkv, tq, tk, seq, tm):
    t, d = x2.shape
    nps = seq // tm
    hw = MLA_HEADS * LANE
    row = lambda w: pl.BlockSpec((tm, w), lambda i: (i, 0))
    tab = pl.BlockSpec((tm, LANE), lambda i: (i % nps, 0))
    return pl.pallas_call(
        _proj_kernel,
        grid=(t // tm,),
        in_specs=[row(d), pl.BlockSpec((1, N_MOD_ROWS, d), lambda i: (i // nps, 0, 0)), _full(gn.shape),
                  _full(win.shape), _full(gq.shape), _full(wq.shape), _full(gkv.shape), _full(wkv.shape),
                  tab, tab],
        out_specs=[row(POOL_WIDTH), row(hw), row(hw), row(hw)],
        out_shape=[jax.ShapeDtypeStruct((t, POOL_WIDTH), F32), jax.ShapeDtypeStruct((t, hw), BF16),
                   jax.ShapeDtypeStruct((t, hw), BF16), jax.ShapeDtypeStruct((t, hw), BF16)],
        compiler_params=_params("arbitrary"),
        name="proj",
    )(x2, mod, gn, win, gq, wq, gkv, wkv, tq, tk)


def _proj_ctx(c2, modc, gn, win, gkv, wkv, tk, tm):
    t, d = c2.shape
    hw = MLA_HEADS * LANE
    row = lambda w: pl.BlockSpec((tm, w), lambda i: (i, 0))
    return pl.pallas_call(
        _proj_ctx_kernel,
        grid=(t // tm,),
        in_specs=[row(d), _full(modc.shape), _full(gn.shape), _full(win.shape), _full(gkv.shape),
                  _full(wkv.shape), _full(tk.shape)],
        out_specs=[row(hw), row(hw)],
        out_shape=[jax.ShapeDtypeStruct((t, hw), BF16), jax.ShapeDtypeStruct((t, hw), BF16)],
        compiler_params=_params("arbitrary"),
        name="proj_ctx",
    )(c2, modc, gn, win, gkv, wkv, tk)


def _pool_kernel(p_ref, w_ref, sc_ref, o_ref):
    n = p_ref.shape[1]
    t = lax.broadcasted_iota(jnp.int32, (n, POOL_GROUP), 0)

    def shifted(a, d):
        r = pltpu.roll(a, (-d) % n, axis=0)
        return jnp.where((t + d >= 0) & (t + d < n), r, 0.0)

    for g, w in enumerate(POOL_WINDOWS):
        half = w // 2
        sl = slice(g * POOL_GROUP, (g + 1) * POOL_GROUP)
        pg = p_ref[0, :, sl]
        fwd, bwd, k = pg, shifted(pg, -1), 1
        while k < half:
            fwd = fwd + shifted(fwd, k)
            bwd = bwd + shifted(bwd, -k)
            k *= 2
        cnt = (jnp.minimum(t + half, n) - jnp.maximum(t - half, 0)).astype(F32)
        diff = (fwd + bwd) / cnt - pg
        y = _dot(diff.astype(BF16), w_ref[g].astype(BF16)) * sc_ref[:, sl]
        o_ref[0, :, sl] = y.astype(BF16)


def _pool(p3, pool_w, pool_scale):
    b, s, c = p3.shape
    return pl.pallas_call(
        _pool_kernel,
        grid=(b,),
        in_specs=[pl.BlockSpec((1, s, c), lambda i: (i, 0, 0)), _full(pool_w.shape), _full((1, c))],
        out_specs=pl.BlockSpec((1, s, c), lambda i: (i, 0, 0)),
        out_shape=jax.ShapeDtypeStruct((b, s, c), BF16),
        compiler_params=_params("arbitrary"),
        name="pool",
    )(p3, pool_w, pool_scale.reshape(1, c))


def _attn_kernel(q_ref, kl_ref, kc_ref, vl_ref, vc_ref, o_ref):
    nt = (((1,), (1,)), ((), ()))
    for r in range(q_ref.shape[0] // MXU_ROWS_PER_DOT):
        rows = slice(r * MXU_ROWS_PER_DOT, (r + 1) * MXU_ROWS_PER_DOT)
        q = q_ref[rows, :]
        s1 = lax.dot_general(q, kl_ref[...], nt, preferred_element_type=F32)
        s2 = lax.dot_general(q, kc_ref[...], nt, preferred_element_type=F32)
        m = jnp.maximum(jnp.max(s1, axis=-1, keepdims=True), jnp.max(s2, axis=-1, keepdims=True))
        p1 = jnp.exp(s1 - m)
        p2 = jnp.exp(s2 - m)
        den = jnp.sum(p1, axis=-1, keepdims=True) + jnp.sum(p2, axis=-1, keepdims=True)
        o = _dot(p1.astype(BF16), vl_ref[...]) + _dot(p2.astype(BF16), vc_ref[...])
        o_ref[rows, :] = (o / den).astype(BF16)


def _attn(q, kl, kc, vl, vc, batch, seq, nctx, tq):
    t, hw = q.shape
    nq = seq // tq
    return pl.pallas_call(
        _attn_kernel,
        grid=(batch, MLA_HEADS, nq),
        in_specs=[pl.BlockSpec((tq, LANE), lambda b, h, i: (b * nq + i, h)),
                  pl.BlockSpec((seq, LANE), lambda b, h, i: (b, h)),
                  pl.BlockSpec((nctx, LANE), lambda b, h, i: (b, h)),
                  pl.BlockSpec((seq, LANE), lambda b, h, i: (b, h)),
                  pl.BlockSpec((nctx, LANE), lambda b, h, i: (b, h))],
        out_specs=pl.BlockSpec((tq, LANE), lambda b, h, i: (b * nq + i, h)),
        out_shape=jax.ShapeDtypeStruct((t, hw), BF16),
        compiler_params=_params("arbitrary", "arbitrary", "arbitrary"),
        name="attn",
    )(q, kl, kc, vl, vc)


def _mix_kernel(pool_ref, mla_ref, wp_ref, wm_ref, x_ref, mod_ref, gn_ref, x1_ref, h2t_ref):
    for rows in _row_chunks(x_ref):
        mix = _dot(pool_ref[rows, :], wp_ref[...]) + _dot(mla_ref[rows, :], wm_ref[...])
        x1 = x_ref[rows, :] + mod_ref[0, 2:3, :] * mix
        x1_ref[rows, :] = x1
        h2 = _rms(x1, gn_ref[...]) * (1.0 + mod_ref[0, 4:5, :]) + mod_ref[0, 3:4, :]
        h2t_ref[:, rows] = h2.T.astype(BF16)


def _mix(pool2, mla2, wp, wm, x2, mod, gn, seq, tm):
    t, d = x2.shape
    nps = seq // tm
    row = lambda w: pl.BlockSpec((tm, w), lambda i: (i, 0))
    return pl.pallas_call(
        _mix_kernel,
        grid=(t // tm,),
        in_specs=[row(pool2.shape[1]), row(mla2.shape[1]), _full(wp.shape), _full(wm.shape), row(d),
                  pl.BlockSpec((1, N_MOD_ROWS, d), lambda i: (i // nps, 0, 0)), _full(gn.shape)],
        out_specs=[row(d), pl.BlockSpec((d, tm), lambda i: (0, i))],
        out_shape=[jax.ShapeDtypeStruct((t, d), F32), jax.ShapeDtypeStruct((d, t), BF16)],
        compiler_params=_params("arbitrary"),
        name="mix",
    )(pool2, mla2, wp, wm, x2, mod, gn)


def _staircase():
    return [(a, b) for a in range(N_RANK) for b in range(N_RANK) if (a + 1) * (b + 1) <= N_RANK]


def _sorting_network(n):
    def merge(lo, hi, r):
        step = r * 2
        if step < hi - lo:
            yield from merge(lo, hi, step)
            yield from merge(lo + r, hi, step)
            yield from ((i, i + r) for i in range(lo + r, hi - r, step))
        else:
            yield (lo, lo + r)

    def sort(lo, hi):
        if hi - lo >= 1:
            mid = lo + (hi - lo) // 2
            yield from sort(lo, mid)
            yield from sort(mid + 1, hi)
            yield from merge(lo, hi, 1)

    return list(sort(0, n - 1))


def _pprep_kernel(h2t_ref, wq_ref, keys_ref, a_ref, cnt_ref, r2_ref, b_ref, qp_scr, top_scr, s_scr):
    qp_scr[...] = _dot(wq_ref[...], h2t_ref[...]).astype(BF16)
    neg = -jnp.inf
    for h in range(PEER_HEADS):
        for p in range(2):
            row0 = (h * 2 + p) * PEER_HALF
            s = _dot(keys_ref[p], qp_scr[row0:row0 + PEER_HALF, :])
            s_scr[p, h] = s
            for c0 in range(0, s.shape[1], LANE):
                cols = slice(c0, c0 + LANE)
                col = [s[g * SUBLANE:(g + 1) * SUBLANE, cols] for g in range(PEER_N_KEYS // SUBLANE)]
                for i, j in _sorting_network(len(col)):
                    col[i], col[j] = jnp.maximum(col[i], col[j]), jnp.minimum(col[i], col[j])
                for r in range(N_RANK):
                    m = jnp.max(col[0], axis=0, keepdims=True)
                    top_scr[p, r, h:h + 1, cols] = m
                    hit = col[0] == m
                    for d in range(N_RANK - 1 - r):
                        col[d] = jnp.where(hit, col[d + 1], col[d])
    v1 = [top_scr[0, r] for r in range(N_RANK)]
    v2 = [top_scr[1, r] for r in range(N_RANK)]
    pairs = _staircase()
    cand = [v1[a] + v2[b] for a, b in pairs]
    work = [[c for c, (a, _) in zip(cand, pairs) if a == r] for r in range(N_RANK)]
    for r in range(PEER_TOPK):
        kth = functools.reduce(jnp.maximum, [row[0] for row in work])
        if r + 1 < PEER_TOPK:
            for row in work:
                hit = row[0] == kth
                row[:] = [jnp.where(hit, nxt, cur) for cur, nxt in zip(row, row[1:] + [neg])]
    cmax = v1[0] + v2[0]
    keep = [c >= kth for c in cand]
    den = sum(jnp.where(k, jnp.exp(c - cmax), 0.0) for k, c in zip(keep, cand))
    inv = math.sqrt(0.5) / den
    width = [sum(jnp.where(k, 1.0, 0.0) for k, (a, _) in zip(keep, pairs) if a == r) for r in range(N_RANK)]
    for h in range(PEER_HEADS):
        hs = slice(h, h + 1)
        for c0 in range(0, s_scr.shape[-1], LANE):
            cols = slice(c0, c0 + LANE)
            s1 = s_scr[0, h, :, cols]
            cnt = jnp.zeros(s1.shape, F32)
            for r in range(N_RANK):
                cnt = jnp.where(s1 == v1[r][hs, cols], width[r][hs, cols], cnt)
            cnt_ref[h, :, cols] = cnt
            a_ref[h, :, cols] = jnp.exp(s1 - v1[0][hs, cols]) * inv[hs, cols]
            s2 = s_scr[1, h, :, cols]
            rank = jnp.full(s2.shape, float(N_RANK), F32)
            for r in range(N_RANK):
                rank = jnp.where(s2 == v2[r][hs, cols], float(r), rank)
            s_scr[0, h, :, cols] = jnp.exp(s2 - v2[0][hs, cols])
            s_scr[1, h, :, cols] = rank
        b_ref[h] = s_scr[0, h].astype(BF16)
        r2_ref[h] = s_scr[1, h].astype(BF16)


def _pprep(h2t, wqt, keys, tn):
    d, t = h2t.shape
    blk = pl.BlockSpec((PEER_HEADS, PEER_N_KEYS, tn), lambda j: (0, 0, j))
    shp = lambda dt: jax.ShapeDtypeStruct((PEER_HEADS, PEER_N_KEYS, t), dt)
    return pl.pallas_call(
        _pprep_kernel,
        grid=(t // tn,),
        in_specs=[pl.BlockSpec((d, tn), lambda j: (0, j)), _full(wqt.shape), _full(keys.shape)],
        out_specs=[blk, blk, blk, blk],
        out_shape=[shp(F32), shp(F32), shp(BF16), shp(BF16)],
        scratch_shapes=[pltpu.VMEM((wqt.shape[0], tn), BF16),
                        pltpu.VMEM((2, N_RANK, PEER_HEADS, tn), F32),
                        pltpu.VMEM((2, PEER_HEADS, PEER_N_KEYS, tn), F32)],
        compiler_params=_params("arbitrary"),
        name="pprep",
    )(h2t, wqt, keys)


def _pdense_kernel(h2t_ref, u_ref, vt_ref, a_ref, cnt_ref, r2_ref, b_ref, x1_ref, mod_ref, o_ref,
                   acc_ref, g_ref, *, n_i1):
    e = pl.program_id(1)

    @pl.when(e == 0)
    def _():
        acc_ref[...] = jnp.zeros_like(acc_ref)

    shape = (PEER_N_KEYS, h2t_ref.shape[1])
    rows = MXU_ROWS_PER_DOT // PEER_N_KEYS
    for i in range(n_i1):
        if i % rows == 0:
            act = _dot(u_ref[i * PEER_N_KEYS:(i + rows) * PEER_N_KEYS, :], h2t_ref[...])
        w = None
        for h in range(PEER_HEADS):
            cnt = jnp.broadcast_to(cnt_ref[h, i:i + 1, :], shape).astype(BF16)
            gate = jnp.broadcast_to(a_ref[h, i:i + 1, :], shape).astype(BF16)
            term = jnp.where(r2_ref[h] < cnt, b_ref[h], jnp.zeros((), BF16)) * gate
            w = term if w is None else w + term
        z = act[(i % rows) * PEER_N_KEYS:(i % rows + 1) * PEER_N_KEYS, :]
        gelu = z * (1.0 + lax.erf(z))
        g_ref[i * PEER_N_KEYS:(i + 1) * PEER_N_KEYS, :] = gelu.astype(BF16) * w
    acc_ref[...] += _dot(vt_ref[...], g_ref[...])

    @pl.when(e == pl.num_programs(1) - 1)
    def _():
        o_ref[...] = x1_ref[...] + mod_ref[0, 5:6, :] * acc_ref[...].T


def _pdense(h2t, u, vt, a, cnt, r2, b, x1, mod, seq, tn, n_i1):
    d, t = h2t.shape
    ne = u.shape[0]
    eb = n_i1 * PEER_N_KEYS
    nps = seq // tn
    gate = pl.BlockSpec((PEER_HEADS, n_i1, tn), lambda j, e: (0, e, j))
    full = pl.BlockSpec((PEER_HEADS, PEER_N_KEYS, tn), lambda j, e: (0, 0, j))
    return pl.pallas_call(
        functools.partial(_pdense_kernel, n_i1=n_i1),
        grid=(t // tn, ne // eb),
        in_specs=[pl.BlockSpec((d, tn), lambda j, e: (0, j)),
                  pl.BlockSpec((eb, d), lambda j, e: (e, 0)),
                  pl.BlockSpec((d, eb), lambda j, e: (0, e)),
                  gate, gate, full, full,
                  pl.BlockSpec((tn, d), lambda j, e: (j, 0)),
                  pl.BlockSpec((1, N_MOD_ROWS, d), lambda j, e: (j // nps, 0, 0))],
        out_specs=pl.BlockSpec((tn, d), lambda j, e: (j, 0)),
        out_shape=jax.ShapeDtypeStruct((t, d), F32),
        scratch_shapes=[pltpu.VMEM((d, tn), F32), pltpu.VMEM((eb, tn), BF16)],
        compiler_params=_params("arbitrary", "arbitrary", vmem=PDENSE_VMEM_LIMIT),
        name="pdense",
    )(h2t, u, vt, a, cnt, r2, b, x1, mod)


def _rot_matrix():
    n = QK_ROPE // 4
    r = np.zeros((QK_ROPE, QK_ROPE), np.float32)
    for axis in range(2):
        for i in range(n):
            r[axis * 2 * n + n + i, axis * 2 * n + i] = -1.0
            r[axis * 2 * n + i, axis * 2 * n + n + i] = 1.0
    return jnp.asarray(r)


def _rope_angles(seq):
    pos = jnp.arange(seq)
    row = (pos // GRID_W).astype(F32)
    col = (pos % GRID_W).astype(F32)
    n = QK_ROPE // 4
    inv = 1.0 / (ROPE_THETA ** (jnp.arange(n, dtype=F32) / n))
    ang_r = row[:, None] * inv
    ang_c = col[:, None] * inv
    return jnp.concatenate([ang_r, ang_r, ang_c, ang_c], axis=-1)


def _head_table(gain, cos, sin, scale):
    rows = cos.shape[0]
    nope = jnp.broadcast_to(gain[:QK_NOPE], (rows, QK_NOPE))
    return jnp.concatenate([nope, gain[QK_NOPE:] * cos, sin], axis=-1) * scale


def _slab_weights(w, gain, width):
    fan_in = w.shape[0]
    wh = w.reshape(fan_in, MLA_HEADS, width)
    nope, rope = wh[..., :QK_NOPE], wh[..., QK_NOPE:QK_HEAD]
    rot = jnp.einsum('khr,rs->khs', rope * gain[QK_NOPE:], _rot_matrix())
    return jnp.concatenate([nope, rope, rot], axis=-1).reshape(fan_in, MLA_HEADS * LANE)


def kernel(x, c, ctx, c_ctx, w_ada, b_ada, g_norm1, w_in, pool_w, pool_scale, g_q_lora, w_q_up, g_kv_lora,
           w_kv_up, g_qk_q, g_qk_k, w_out, g_norm2, peer_w_q, peer_sub_keys, peer_u, peer_v):
    bsz, seq, d = x.shape
    nctx = ctx.shape[1]
    t = bsz * seq
    o1, o2, o3 = POOL_WIDTH, POOL_WIDTH + Q_LORA, POOL_WIDTH + Q_LORA + KV_LORA

    rows = -(-(bsz + 1) // 16) * 16
    cvec = jnp.zeros((rows, d), F32).at[:bsz].set(c).at[bsz].set(c_ctx)
    mod_all = _ada(cvec, w_ada, b_ada).reshape(rows, 6, d)
    mod_all = jnp.pad(mod_all, ((0, 0), (0, N_MOD_ROWS - 6), (0, 0)))
    mod, modc = mod_all[:bsz], mod_all[bsz:bsz + 1]

    zeros = lambda r, cdim: jnp.zeros((r, cdim), F32)
    w_kr = w_in[:, o3:]
    w_krx = jnp.concatenate([zeros(d, QK_NOPE), w_kr, (w_kr * g_qk_k[QK_NOPE:]) @ _rot_matrix()], axis=-1)
    win = jnp.concatenate([w_in[:, :o3], w_krx], axis=-1).astype(BF16)
    win_ctx = jnp.concatenate([w_in[:, o2:o3], w_krx], axis=-1).astype(BF16)
    wq = _slab_weights(w_q_up, g_qk_q, QK_HEAD).astype(BF16)
    wkv_h = w_kv_up.reshape(KV_LORA, MLA_HEADS, QK_NOPE + V_HEAD)
    pad = jnp.zeros((KV_LORA, MLA_HEADS, LANE - QK_NOPE), F32)
    wk = jnp.concatenate([wkv_h[..., :QK_NOPE], pad], axis=-1).reshape(KV_LORA, MLA_HEADS * LANE)
    wv = jnp.concatenate([wkv_h[..., QK_NOPE:], pad], axis=-1).reshape(KV_LORA, MLA_HEADS * LANE)
    wkv = jnp.concatenate([wk, wv], axis=-1).astype(BF16)
    w_out_mla = w_out[POOL_WIDTH:].reshape(MLA_HEADS, V_HEAD, d)
    w_out_mla = jnp.pad(w_out_mla, ((0, 0), (0, LANE - V_HEAD), (0, 0))).reshape(MLA_HEADS * LANE, d).astype(BF16)
    w_out_pool = w_out[:POOL_WIDTH].astype(BF16)

    ang = _rope_angles(seq)
    tq = _head_table(g_qk_q, jnp.cos(ang), jnp.sin(ang), 1.0 / math.sqrt(QK_HEAD))
    tk = _head_table(g_qk_k, jnp.cos(ang), jnp.sin(ang), 1.0)
    tk_ctx = _head_table(g_qk_k, jnp.ones((8, QK_ROPE), F32), jnp.zeros((8, QK_ROPE), F32), 1.0)[:1]

    row = lambda v: v.reshape(1, -1)
    x2 = x.reshape(t, d)
    tm = min(TOKEN_TILE, seq)
    p_in, q, k_lat, v_lat = _proj(x2, mod, row(g_norm1), win, row(g_q_lora), wq, row(g_kv_lora), wkv,
                                  tq, tk, seq, tm)
    k_ctx, v_ctx = _proj_ctx(ctx.reshape(bsz * nctx, d), modc, row(g_norm1), win_ctx, row(g_kv_lora), wkv,
                             tk_ctx, nctx)
    pool_out = _pool(p_in.reshape(bsz, seq, POOL_WIDTH), pool_w, pool_scale)
    mla_out = _attn(q, k_lat, k_ctx, v_lat, v_ctx, bsz, seq, nctx, seq)
    x1, h2t = _mix(pool_out.reshape(t, POOL_WIDTH), mla_out, w_out_pool, w_out_mla, x2, mod, row(g_norm2),
                   seq, tm)

    a, cnt, r2, b = _pprep(h2t, peer_w_q.T.astype(BF16), peer_sub_keys.astype(BF16), PPREP_TOKEN_TILE)
    u_scaled = (peer_u * math.sqrt(0.5)).astype(BF16)
    out = _pdense(h2t, u_scaled, peer_v.T.astype(BF16), a, cnt, r2, b, x1, mod, seq, tm, PDENSE_KEYS_PER_STEP)
    return out.reshape(bsz, seq, d)
```

```python
import functools
import math

import jax
import jax.numpy as jnp
import numpy as np
from jax import lax
from jax.experimental import pallas as pl
from jax.experimental.pallas import tpu as pltpu

GRID_W = 64
POOL_WINDOWS = (2, 4, 8, 16)
POOL_GROUP = 128
POOL_WIDTH = POOL_GROUP * len(POOL_WINDOWS)
MLA_HEADS = 8
QK_NOPE = 64
QK_ROPE = 32
QK_HEAD = QK_NOPE + QK_ROPE
V_HEAD = 64
Q_LORA = 384
KV_LORA = 256
ROPE_THETA = 10000.0
PEER_HEADS = 8
PEER_N_KEYS = 128
PEER_HALF = 128
PEER_TOPK = 16
EPS = 1e-6

LANE = 128
SUBLANE = 8
N_MOD_ROWS = 8
N_RANK = PEER_TOPK
VMEM_LIMIT = 48 * 1024 * 1024
PDENSE_VMEM_LIMIT = 58 * 1024 * 1024
MXU_ROWS_PER_DOT = 256
ROW_TILE = 1024
TOKEN_TILE = 512
PPREP_TOKEN_TILE = 256
PDENSE_ROWS_PER_DOT = 512
PDENSE_KEYS_PER_STEP = 32

BF16 = jnp.bfloat16
F32 = jnp.float32


def _params(*sem, vmem=None):
    return pltpu.CompilerParams(dimension_semantics=sem, vmem_limit_bytes=vmem or VMEM_LIMIT)


def _dot(a, b):
    return jnp.dot(a, b, preferred_element_type=F32)


def _split3(a):
    hi = a.astype(BF16)
    lo = (a - hi.astype(F32)).astype(BF16)
    return hi, lo


def _rms(xf, g):
    ms = jnp.mean(xf * xf, axis=-1, keepdims=True)
    return xf * lax.rsqrt(ms + EPS) * g


def _ada_kernel(c_ref, w_ref, b_ref, o_ref):
    c = c_ref[...]
    a = c / (1.0 + jnp.exp(-c))
    a_hi, a_lo = _split3(a)
    w_hi, w_lo = _split3(w_ref[...])
    o_ref[...] = _dot(a_hi, w_hi) + _dot(a_hi, w_lo) + _dot(a_lo, w_hi) + b_ref[...]


def _ada(cvec, w_ada, b_ada):
    rows, d = cvec.shape
    n = w_ada.shape[1]
    tn = 1536
    return pl.pallas_call(
        _ada_kernel,
        grid=(n // tn,),
        in_specs=[pl.BlockSpec((rows, d), lambda j: (0, 0)),
                  pl.BlockSpec((d, tn), lambda j: (0, j)),
                  pl.BlockSpec((1, tn), lambda j: (0, j))],
        out_specs=pl.BlockSpec((rows, tn), lambda j: (0, j)),
        out_shape=jax.ShapeDtypeStruct((rows, n), F32),
        compiler_params=_params("arbitrary"),
        name="ada",
    )(cvec, w_ada, b_ada.reshape(1, n))


def _head_norm_rope(slab, table):
    lane = lax.broadcasted_iota(jnp.int32, slab.shape, 1)
    seg = (lax.broadcasted_iota(jnp.int32, (LANE, LANE), 0) < QK_HEAD).astype(BF16)
    ss = _dot((slab * slab).astype(BF16), seg)
    t = slab * lax.rsqrt(ss * (1.0 / QK_HEAD) + EPS) * table
    r = pltpu.roll(t, LANE - QK_ROPE, axis=1)
    return jnp.where(lane < QK_NOPE, t, jnp.where(lane < QK_HEAD, t + r, 0.0))


def _modulated(x, mod_ref, gn_ref):
    h = _rms(x, gn_ref[...])
    return (h * (1.0 + mod_ref[0, 1:2, :]) + mod_ref[0, 0:1, :]).astype(BF16)


def _row_chunks(ref):
    n = ref.shape[0]
    step = min(MXU_ROWS_PER_DOT, n)
    return [slice(r, r + step) for r in range(0, n, step)]


def _keys_values(ckv, krx, gkv_ref, wkv_ref, tk, k_ref, v_ref, rows):
    kvx = _dot(_rms(ckv, gkv_ref[...]).astype(BF16), wkv_ref[...])
    hw = MLA_HEADS * LANE
    for h in range(MLA_HEADS):
        slab = kvx[:, h * LANE:(h + 1) * LANE] + krx
        k_ref[rows, h * LANE:(h + 1) * LANE] = _head_norm_rope(slab, tk).astype(BF16)
    v_ref[rows, :] = kvx[:, hw:2 * hw].astype(BF16)


def _proj_kernel(x_ref, mod_ref, gn_ref, win_ref, gq_ref, wq_ref, gkv_ref, wkv_ref, tq_ref, tk_ref,
                 p_ref, q_ref, k_ref, v_ref):
    o1, o2, o3 = POOL_WIDTH, POOL_WIDTH + Q_LORA, POOL_WIDTH + Q_LORA + KV_LORA
    for rows in _row_chunks(x_ref):
        proj = _dot(_modulated(x_ref[rows, :], mod_ref, gn_ref), win_ref[...])
        p_ref[rows, :] = proj[:, :o1]
        qs = _dot(_rms(proj[:, o1:o2], gq_ref[...]).astype(BF16), wq_ref[...])
        tq = tq_ref[rows, :]
        for h in range(MLA_HEADS):
            q_ref[rows, h * LANE:(h + 1) * LANE] = _head_norm_rope(qs[:, h * LANE:(h + 1) * LANE], tq).astype(BF16)
        _keys_values(proj[:, o2:o3], proj[:, o3:o3 + LANE], gkv_ref, wkv_ref, tk_ref[rows, :], k_ref, v_ref, rows)


def _proj_ctx_kernel(x_ref, mod_ref, gn_ref, win_ref, gkv_ref, wkv_ref, tk_ref, k_ref, v_ref):
    for rows in _row_chunks(x_ref):
        proj = _dot(_modulated(x_ref[rows, :], mod_ref, gn_ref), win_ref[...])
        _keys_values(proj[:, :KV_LORA], proj[:, KV_LORA:KV_LORA + LANE], gkv_ref, wkv_ref, tk_ref[...], k_ref, v_ref,
                     rows)


def _full(shape):
    return pl.BlockSpec(shape, lambda *_: (0,) * len(shape))


def _proj(x2, mod, gn, win, gq, wq, gkv, wkv, tq, tk, seq, tm):
    t, d = x2.shape
    nps = seq // tm
    hw = MLA_HEADS * LANE
    row = lambda w: pl.BlockSpec((tm, w), lambda i: (i, 0))
    tab = pl.BlockSpec((tm, LANE), lambda i: (i % nps, 0))
    return pl.pallas_call(
        _proj_kernel,
        grid=(t // tm,),
        in_specs=[row(d), pl.BlockSpec((1, N_MOD_ROWS, d), lambda i: (i // nps, 0, 0)), _full(gn.shape),
                  _full(win.shape), _full(gq.shape), _full(wq.shape), _full(gkv.shape), _full(wkv.shape),
                  tab, tab],
        out_specs=[row(POOL_WIDTH), row(hw), row(hw), row(hw)],
        out_shape=[jax.ShapeDtypeStruct((t, POOL_WIDTH), F32), jax.ShapeDtypeStruct((t, hw), BF16),
                   jax.ShapeDtypeStruct((t, hw), BF16), jax.ShapeDtypeStruct((t, hw), BF16)],
        compiler_params=_params("arbitrary"),
        name="proj",
    )(x2, mod, gn, win, gq, wq, gkv, wkv, tq, tk)


def _proj_ctx(c2, modc, gn, win, gkv, wkv, tk, tm):
    t, d = c2.shape
    hw = MLA_HEADS * LANE
    row = lambda w: pl.BlockSpec((tm, w), lambda i: (i, 0))
    return pl.pallas_call(
        _proj_ctx_kernel,
        grid=(t // tm,),
        in_specs=[row(d), _full(modc.shape), _full(gn.shape), _full(win.shape), _full(gkv.shape),
                  _full(wkv.shape), _full(tk.shape)],
        out_specs=[row(hw), row(hw)],
        out_shape=[jax.ShapeDtypeStruct((t, hw), BF16), jax.ShapeDtypeStruct((t, hw), BF16)],
        compiler_params=_params("arbitrary"),
        name="proj_ctx",
    )(c2, modc, gn, win, gkv, wkv, tk)


def _pool_kernel(p_ref, w_ref, sc_ref, o_ref):
    n = p_ref.shape[1]
    t = lax.broadcasted_iota(jnp.int32, (n, POOL_GROUP), 0)

    def shifted(a, d):
        r = pltpu.roll(a, (-d) % n, axis=0)
        return jnp.where((t + d >= 0) & (t + d < n), r, 0.0)

    for g, w in enumerate(POOL_WINDOWS):
        half = w // 2
        sl = slice(g * POOL_GROUP, (g + 1) * POOL_GROUP)
        pg = p_ref[0, :, sl]
        fwd, bwd, k = pg, shifted(pg, -1), 1
        while k < half:
            fwd = fwd + shifted(fwd, k)
            bwd = bwd + shifted(bwd, -k)
            k *= 2
        cnt = (jnp.minimum(t + half, n) - jnp.maximum(t - half, 0)).astype(F32)
        diff = (fwd + bwd) / cnt - pg
        y = _dot(diff.astype(BF16), w_ref[g].astype(BF16)) * sc_ref[:, sl]
        o_ref[0, :, sl] = y.astype(BF16)


def _pool(p3, pool_w, pool_scale):
    b, s, c = p3.shape
    return pl.pallas_call(
        _pool_kernel,
        grid=(b,),
        in_specs=[pl.BlockSpec((1, s, c), lambda i: (i, 0, 0)), _full(pool_w.shape), _full((1, c))],
        out_specs=pl.BlockSpec((1, s, c), lambda i: (i, 0, 0)),
        out_shape=jax.ShapeDtypeStruct((b, s, c), BF16),
        compiler_params=_params("arbitrary"),
        name="pool",
    )(p3, pool_w, pool_scale.reshape(1, c))


def _attn_kernel(q_ref, kl_ref, kc_ref, vl_ref, vc_ref, o_ref):
    nt = (((1,), (1,)), ((), ()))
    for rows in _row_chunks(q_ref):
        q = q_ref[rows, :]
        s1 = lax.dot_general(q, kl_ref[...], nt, preferred_element_type=F32)
        s2 = lax.dot_general(q, kc_ref[...], nt, preferred_element_type=F32)
        m = jnp.maximum(jnp.max(s1, axis=-1, keepdims=True), jnp.max(s2, axis=-1, keepdims=True))
        p1 = jnp.exp2(s1 - m)
        p2 = jnp.exp2(s2 - m)
        den = jnp.sum(p1, axis=-1, keepdims=True) + jnp.sum(p2, axis=-1, keepdims=True)
        o = _dot(p1.astype(BF16), vl_ref[...]) + _dot(p2.astype(BF16), vc_ref[...])
        o_ref[rows, :] = (o / den).astype(BF16)


def _attn(q, kl, kc, vl, vc, batch, seq, nctx, tq):
    t, hw = q.shape
    nq = seq // tq
    return pl.pallas_call(
        _attn_kernel,
        grid=(batch, MLA_HEADS, nq),
        in_specs=[pl.BlockSpec((tq, LANE), lambda b, h, i: (b * nq + i, h)),
                  pl.BlockSpec((seq, LANE), lambda b, h, i: (b, h)),
                  pl.BlockSpec((nctx, LANE), lambda b, h, i: (b, h)),
                  pl.BlockSpec((seq, LANE), lambda b, h, i: (b, h)),
                  pl.BlockSpec((nctx, LANE), lambda b, h, i: (b, h))],
        out_specs=pl.BlockSpec((tq, LANE), lambda b, h, i: (b * nq + i, h)),
        out_shape=jax.ShapeDtypeStruct((t, hw), BF16),
        compiler_params=_params("arbitrary", "arbitrary", "arbitrary"),
        name="attn",
    )(q, kl, kc, vl, vc)


def _mix_kernel(pool_ref, mla_ref, wp_ref, wm_ref, x_ref, mod_ref, gn_ref, x1_ref, h2t_ref):
    for rows in _row_chunks(x_ref):
        mix = _dot(pool_ref[rows, :], wp_ref[...]) + _dot(mla_ref[rows, :], wm_ref[...])
        x1 = x_ref[rows, :] + mod_ref[0, 2:3, :] * mix
        x1_ref[rows, :] = x1
        h2 = _rms(x1, gn_ref[...]) * (1.0 + mod_ref[0, 4:5, :]) + mod_ref[0, 3:4, :]
        h2t_ref[:, rows] = h2.T.astype(BF16)


def _mix(pool2, mla2, wp, wm, x2, mod, gn, seq, tm):
    t, d = x2.shape
    nps = seq // tm
    row = lambda w: pl.BlockSpec((tm, w), lambda i: (i, 0))
    return pl.pallas_call(
        _mix_kernel,
        grid=(t // tm,),
        in_specs=[row(pool2.shape[1]), row(mla2.shape[1]), _full(wp.shape), _full(wm.shape), row(d),
                  pl.BlockSpec((1, N_MOD_ROWS, d), lambda i: (i // nps, 0, 0)), _full(gn.shape)],
        out_specs=[row(d), pl.BlockSpec((d, tm), lambda i: (0, i))],
        out_shape=[jax.ShapeDtypeStruct((t, d), F32), jax.ShapeDtypeStruct((d, t), BF16)],
        compiler_params=_params("arbitrary"),
        name="mix",
    )(pool2, mla2, wp, wm, x2, mod, gn)


def _staircase():
    return [(a, b) for a in range(N_RANK) for b in range(N_RANK) if (a + 1) * (b + 1) <= N_RANK]


def _sorting_network(n):
    def merge(lo, hi, r):
        step = r * 2
        if step < hi - lo:
            yield from merge(lo, hi, step)
            yield from merge(lo + r, hi, step)
            yield from ((i, i + r) for i in range(lo + r, hi - r, step))
        else:
            yield (lo, lo + r)

    def sort(lo, hi):
        if hi - lo >= 1:
            mid = lo + (hi - lo) // 2
            yield from sort(lo, mid)
            yield from sort(mid + 1, hi)
            yield from merge(lo, hi, 1)

    return list(sort(0, n - 1))


def _pprep_kernel(h2t_ref, wq_ref, keys_ref, a_ref, cnt_ref, r2_ref, b_ref, qp_scr, top_scr, s_scr):
    qp_scr[...] = _dot(wq_ref[...], h2t_ref[...]).astype(BF16)
    neg = -jnp.inf
    for h in range(PEER_HEADS):
        for p in range(2):
            row0 = (h * 2 + p) * PEER_HALF
            s = _dot(keys_ref[p], qp_scr[row0:row0 + PEER_HALF, :])
            s_scr[p, h] = s
            for c0 in range(0, s.shape[1], LANE):
                cols = slice(c0, c0 + LANE)
                col = [s[g * SUBLANE:(g + 1) * SUBLANE, cols] for g in range(PEER_N_KEYS // SUBLANE)]
                for i, j in _sorting_network(len(col)):
                    col[i], col[j] = jnp.maximum(col[i], col[j]), jnp.minimum(col[i], col[j])
                for r in range(N_RANK):
                    m = jnp.max(col[0], axis=0, keepdims=True)
                    top_scr[p, r, h:h + 1, cols] = m
                    hit = col[0] == m
                    for d in range(N_RANK - 1 - r):
                        col[d] = jnp.where(hit, col[d + 1], col[d])
    v1 = [top_scr[0, r] for r in range(N_RANK)]
    v2 = [top_scr[1, r] for r in range(N_RANK)]
    pairs = _staircase()
    cand = [v1[a] + v2[b] for a, b in pairs]
    work = [[c for c, (a, _) in zip(cand, pairs) if a == r] for r in range(N_RANK)]
    for r in range(PEER_TOPK):
        kth = functools.reduce(jnp.maximum, [row[0] for row in work])
        if r + 1 < PEER_TOPK:
            for row in work:
                hit = row[0] == kth
                row[:] = [jnp.where(hit, nxt, cur) for cur, nxt in zip(row, row[1:] + [neg])]
    cmax = v1[0] + v2[0]
    keep = [c >= kth for c in cand]
    den = sum(jnp.where(k, jnp.exp(c - cmax), 0.0) for k, c in zip(keep, cand))
    inv = math.sqrt(0.5) / den
    width = [sum(jnp.where(k, 1.0, 0.0) for k, (a, _) in zip(keep, pairs) if a == r) for r in range(N_RANK)]
    for h in range(PEER_HEADS):
        hs = slice(h, h + 1)
        for c0 in range(0, s_scr.shape[-1], LANE):
            cols = slice(c0, c0 + LANE)
            s1 = s_scr[0, h, :, cols]
            cnt = jnp.zeros(s1.shape, F32)
            for r in range(N_RANK):
                cnt = jnp.where(s1 == v1[r][hs, cols], width[r][hs, cols], cnt)
            cnt_ref[h, :, cols] = cnt
            a_ref[h, :, cols] = jnp.exp(s1 - v1[0][hs, cols]) * inv[hs, cols]
            s2 = s_scr[1, h, :, cols]
            rank = jnp.full(s2.shape, float(N_RANK), F32)
            for r in range(N_RANK):
                rank = jnp.where(s2 == v2[r][hs, cols], float(r), rank)
            s_scr[0, h, :, cols] = jnp.exp(s2 - v2[0][hs, cols])
            s_scr[1, h, :, cols] = rank
        b_ref[h] = s_scr[0, h].astype(BF16)
        r2_ref[h] = s_scr[1, h].astype(BF16)


def _pprep(h2t, wqt, keys, tn):
    d, t = h2t.shape
    blk = pl.BlockSpec((PEER_HEADS, PEER_N_KEYS, tn), lambda j: (0, 0, j))
    shp = lambda dt: jax.ShapeDtypeStruct((PEER_HEADS, PEER_N_KEYS, t), dt)
    return pl.pallas_call(
        _pprep_kernel,
        grid=(t // tn,),
        in_specs=[pl.BlockSpec((d, tn), lambda j: (0, j)), _full(wqt.shape), _full(keys.shape)],
        out_specs=[blk, blk, blk, blk],
        out_shape=[shp(F32), shp(F32), shp(BF16), shp(BF16)],
        scratch_shapes=[pltpu.VMEM((wqt.shape[0], tn), BF16),
                        pltpu.VMEM((2, N_RANK, PEER_HEADS, tn), F32),
                        pltpu.VMEM((2, PEER_HEADS, PEER_N_KEYS, tn), F32)],
        compiler_params=_params("arbitrary"),
        name="pprep",
    )(h2t, wqt, keys)


def _pdense_kernel(h2t_ref, u_ref, vt_ref, a_ref, cnt_ref, r2_ref, b_ref, x1_ref, mod_ref, o_ref,
                   acc_ref, g_ref, *, n_i1):
    e = pl.program_id(1)

    @pl.when(e == 0)
    def _():
        acc_ref[...] = jnp.zeros_like(acc_ref)

    shape = (PEER_N_KEYS, h2t_ref.shape[1])
    rows = PDENSE_ROWS_PER_DOT // PEER_N_KEYS
    for i in range(n_i1):
        if i % rows == 0:
            act = _dot(u_ref[i * PEER_N_KEYS:(i + rows) * PEER_N_KEYS, :], h2t_ref[...])
        w = None
        for h in range(PEER_HEADS):
            cnt = jnp.broadcast_to(cnt_ref[h, i:i + 1, :], shape).astype(BF16)
            gate = jnp.broadcast_to(a_ref[h, i:i + 1, :], shape).astype(BF16)
            term = jnp.where(r2_ref[h] < cnt, b_ref[h], jnp.zeros((), BF16)) * gate
            w = term if w is None else w + term
        z = act[(i % rows) * PEER_N_KEYS:(i % rows + 1) * PEER_N_KEYS, :]
        gelu = z * (1.0 + lax.erf(z))
        g_ref[i * PEER_N_KEYS:(i + 1) * PEER_N_KEYS, :] = gelu.astype(BF16) * w
    acc_ref[...] += _dot(vt_ref[...], g_ref[...])

    @pl.when(e == pl.num_programs(1) - 1)
    def _():
        o_ref[...] = x1_ref[...] + mod_ref[0, 5:6, :] * acc_ref[...].T


def _pdense(h2t, u, vt, a, cnt, r2, b, x1, mod, seq, tn, n_i1):
    d, t = h2t.shape
    ne = u.shape[0]
    eb = n_i1 * PEER_N_KEYS
    nps = seq // tn
    gate = pl.BlockSpec((PEER_HEADS, n_i1, tn), lambda j, e: (0, e, j))
    full = pl.BlockSpec((PEER_HEADS, PEER_N_KEYS, tn), lambda j, e: (0, 0, j))
    return pl.pallas_call(
        functools.partial(_pdense_kernel, n_i1=n_i1),
        grid=(t // tn, ne // eb),
        in_specs=[pl.BlockSpec((d, tn), lambda j, e: (0, j)),
                  pl.BlockSpec((eb, d), lambda j, e: (e, 0)),
                  pl.BlockSpec((d, eb), lambda j, e: (0, e)),
                  gate, gate, full, full,
                  pl.BlockSpec((tn, d), lambda j, e: (j, 0)),
                  pl.BlockSpec((1, N_MOD_ROWS, d), lambda j, e: (j // nps, 0, 0))],
        out_specs=pl.BlockSpec((tn, d), lambda j, e: (j, 0)),
        out_shape=jax.ShapeDtypeStruct((t, d), F32),
        scratch_shapes=[pltpu.VMEM((d, tn), F32), pltpu.VMEM((eb, tn), BF16)],
        compiler_params=_params("arbitrary", "arbitrary", vmem=PDENSE_VMEM_LIMIT),
        name="pdense",
    )(h2t, u, vt, a, cnt, r2, b, x1, mod)


def _rot_matrix():
    n = QK_ROPE // 4
    r = np.zeros((QK_ROPE, QK_ROPE), np.float32)
    for axis in range(2):
        for i in range(n):
            r[axis * 2 * n + n + i, axis * 2 * n + i] = -1.0
            r[axis * 2 * n + i, axis * 2 * n + n + i] = 1.0
    return jnp.asarray(r)


def _rope_angles(seq):
    pos = jnp.arange(seq)
    row = (pos // GRID_W).astype(F32)
    col = (pos % GRID_W).astype(F32)
    n = QK_ROPE // 4
    inv = 1.0 / (ROPE_THETA ** (jnp.arange(n, dtype=F32) / n))
    ang_r = row[:, None] * inv
    ang_c = col[:, None] * inv
    return jnp.concatenate([ang_r, ang_r, ang_c, ang_c], axis=-1)


def _head_table(gain, cos, sin, scale):
    rows = cos.shape[0]
    nope = jnp.broadcast_to(gain[:QK_NOPE], (rows, QK_NOPE))
    return jnp.concatenate([nope, gain[QK_NOPE:] * cos, sin], axis=-1) * scale


def _slab_weights(w, gain, width):
    fan_in = w.shape[0]
    wh = w.reshape(fan_in, MLA_HEADS, width)
    nope, rope = wh[..., :QK_NOPE], wh[..., QK_NOPE:QK_HEAD]
    rot = jnp.einsum('khr,rs->khs', rope * gain[QK_NOPE:], _rot_matrix())
    return jnp.concatenate([nope, rope, rot], axis=-1).reshape(fan_in, MLA_HEADS * LANE)


def kernel(x, c, ctx, c_ctx, w_ada, b_ada, g_norm1, w_in, pool_w, pool_scale, g_q_lora, w_q_up, g_kv_lora,
           w_kv_up, g_qk_q, g_qk_k, w_out, g_norm2, peer_w_q, peer_sub_keys, peer_u, peer_v):
    bsz, seq, d = x.shape
    nctx = ctx.shape[1]
    t = bsz * seq
    o1, o2, o3 = POOL_WIDTH, POOL_WIDTH + Q_LORA, POOL_WIDTH + Q_LORA + KV_LORA

    rows = -(-(bsz + 1) // 16) * 16
    cvec = jnp.zeros((rows, d), F32).at[:bsz].set(c).at[bsz].set(c_ctx)
    mod_all = _ada(cvec, w_ada, b_ada).reshape(rows, 6, d)
    mod_all = jnp.pad(mod_all, ((0, 0), (0, N_MOD_ROWS - 6), (0, 0)))
    mod, modc = mod_all[:bsz], mod_all[bsz:bsz + 1]

    zeros = lambda r, cdim: jnp.zeros((r, cdim), F32)
    w_kr = w_in[:, o3:]
    w_krx = jnp.concatenate([zeros(d, QK_NOPE), w_kr, (w_kr * g_qk_k[QK_NOPE:]) @ _rot_matrix()], axis=-1)
    win = jnp.concatenate([w_in[:, :o3], w_krx], axis=-1).astype(BF16)
    win_ctx = jnp.concatenate([w_in[:, o2:o3], w_krx], axis=-1).astype(BF16)
    wq = _slab_weights(w_q_up, g_qk_q, QK_HEAD).astype(BF16)
    wkv_h = w_kv_up.reshape(KV_LORA, MLA_HEADS, QK_NOPE + V_HEAD)
    pad = jnp.zeros((KV_LORA, MLA_HEADS, LANE - QK_NOPE), F32)
    wk = jnp.concatenate([wkv_h[..., :QK_NOPE], pad], axis=-1).reshape(KV_LORA, MLA_HEADS * LANE)
    wv = jnp.concatenate([wkv_h[..., QK_NOPE:], pad], axis=-1).reshape(KV_LORA, MLA_HEADS * LANE)
    wkv = jnp.concatenate([wk, wv], axis=-1).astype(BF16)
    w_out_mla = w_out[POOL_WIDTH:].reshape(MLA_HEADS, V_HEAD, d)
    w_out_mla = jnp.pad(w_out_mla, ((0, 0), (0, LANE - V_HEAD), (0, 0))).reshape(MLA_HEADS * LANE, d).astype(BF16)
    w_out_pool = w_out[:POOL_WIDTH].astype(BF16)

    ang = _rope_angles(seq)
    tq = _head_table(g_qk_q, jnp.cos(ang), jnp.sin(ang), math.log2(math.e) / math.sqrt(QK_HEAD))
    tk = _head_table(g_qk_k, jnp.cos(ang), jnp.sin(ang), 1.0)
    tk_ctx = _head_table(g_qk_k, jnp.ones((8, QK_ROPE), F32), jnp.zeros((8, QK_ROPE), F32), 1.0)[:1]

    row = lambda v: v.reshape(1, -1)
    x2 = x.reshape(t, d)
    tm, tr = min(TOKEN_TILE, seq), min(ROW_TILE, seq)
    p_in, q, k_lat, v_lat = _proj(x2, mod, row(g_norm1), win, row(g_q_lora), wq, row(g_kv_lora), wkv,
                                  tq, tk, seq, tr)
    k_ctx, v_ctx = _proj_ctx(ctx.reshape(bsz * nctx, d), modc, row(g_norm1), win_ctx, row(g_kv_lora), wkv,
                             tk_ctx, nctx)
    pool_out = _pool(p_in.reshape(bsz, seq, POOL_WIDTH), pool_w, pool_scale)
    mla_out = _attn(q, k_lat, k_ctx, v_lat, v_ctx, bsz, seq, nctx, seq)
    x1, h2t = _mix(pool_out.reshape(t, POOL_WIDTH), mla_out, w_out_pool, w_out_mla, x2, mod, row(g_norm2),
                   seq, tr)

    a, cnt, r2, b = _pprep(h2t, peer_w_q.T.astype(BF16), peer_sub_keys.astype(BF16), PPREP_TOKEN_TILE)
    u_scaled = (peer_u * math.sqrt(0.5)).astype(BF16)
    out = _pdense(h2t, u_scaled, peer_v.T.astype(BF16), a, cnt, r2, b, x1, mod, seq, tm, PDENSE_KEYS_PER_STEP)
    return out.reshape(bsz, seq, d)
```

```python
import functools
import math

import jax
import jax.numpy as jnp
import numpy as np
from jax import lax
from jax.experimental import pallas as pl
from jax.experimental.pallas import tpu as pltpu

GRID_W = 64
POOL_WINDOWS = (2, 4, 8, 16)
POOL_GROUP = 128
POOL_WIDTH = POOL_GROUP * len(POOL_WINDOWS)
MLA_HEADS = 8
QK_NOPE = 64
QK_ROPE = 32
QK_HEAD = QK_NOPE + QK_ROPE
V_HEAD = 64
Q_LORA = 384
KV_LORA = 256
ROPE_THETA = 10000.0
PEER_HEADS = 8
PEER_N_KEYS = 128
PEER_HALF = 128
PEER_TOPK = 16
EPS = 1e-6

LANE = 128
SUBLANE = 8
N_MOD_ROWS = 8
N_RANK = PEER_TOPK
VMEM_LIMIT = 48 * 1024 * 1024
PDENSE_VMEM_LIMIT = 58 * 1024 * 1024
MXU_ROWS_PER_DOT = 256
ROW_TILE = 1024
TOKEN_TILE = 512
PPREP_TOKEN_TILE = 256
KEY_BLOCK = 512
PROJ_ROWS_PER_DOT = 512
PDENSE_ROWS_PER_DOT = 512
PDENSE_KEYS_PER_STEP = 32

BF16 = jnp.bfloat16
F32 = jnp.float32


def _params(*sem, vmem=None):
    return pltpu.CompilerParams(dimension_semantics=sem, vmem_limit_bytes=vmem or VMEM_LIMIT)


def _dot(a, b):
    return jnp.dot(a, b, preferred_element_type=F32)


def _split3(a):
    hi = a.astype(BF16)
    lo = (a - hi.astype(F32)).astype(BF16)
    return hi, lo


def _rms(xf, g):
    ms = jnp.mean(xf * xf, axis=-1, keepdims=True)
    return xf * lax.rsqrt(ms + EPS) * g


def _ada_kernel(c_ref, w_ref, b_ref, o_ref):
    c = c_ref[...]
    a = c / (1.0 + jnp.exp(-c))
    a_hi, a_lo = _split3(a)
    w_hi, w_lo = _split3(w_ref[...])
    o_ref[...] = _dot(a_hi, w_hi) + _dot(a_hi, w_lo) + _dot(a_lo, w_hi) + b_ref[...]


def _ada(cvec, w_ada, b_ada):
    rows, d = cvec.shape
    n = w_ada.shape[1]
    tn = 1536
    return pl.pallas_call(
        _ada_kernel,
        grid=(n // tn,),
        in_specs=[pl.BlockSpec((rows, d), lambda j: (0, 0)),
                  pl.BlockSpec((d, tn), lambda j: (0, j)),
                  pl.BlockSpec((1, tn), lambda j: (0, j))],
        out_specs=pl.BlockSpec((rows, tn), lambda j: (0, j)),
        out_shape=jax.ShapeDtypeStruct((rows, n), F32),
        compiler_params=_params("arbitrary"),
        name="ada",
    )(cvec, w_ada, b_ada.reshape(1, n))


def _head_norm_rope(slab, table):
    lane = lax.broadcasted_iota(jnp.int32, slab.shape, 1)
    seg = (lax.broadcasted_iota(jnp.int32, (LANE, LANE), 0) < QK_HEAD).astype(BF16)
    ss = _dot((slab * slab).astype(BF16), seg)
    t = slab * lax.rsqrt(ss * (1.0 / QK_HEAD) + EPS) * table
    r = pltpu.roll(t, LANE - QK_ROPE, axis=1)
    return jnp.where(lane < QK_NOPE, t, jnp.where(lane < QK_HEAD, t + r, 0.0))


def _modulated(x, mod_ref, gn_ref):
    h = _rms(x, gn_ref[...])
    return (h * (1.0 + mod_ref[0, 1:2, :]) + mod_ref[0, 0:1, :]).astype(BF16)


def _row_chunks(ref, rows_per_chunk=MXU_ROWS_PER_DOT):
    n = ref.shape[0]
    step = min(rows_per_chunk, n)
    return [slice(r, r + step) for r in range(0, n, step)]


def _keys_values(ckv, krx, gkv_ref, wkv_ref, tk, k_ref, v_ref, rows):
    kvx = _dot(_rms(ckv, gkv_ref[...]).astype(BF16), wkv_ref[...])
    hw = MLA_HEADS * LANE
    for h in range(MLA_HEADS):
        slab = kvx[:, h * LANE:(h + 1) * LANE] + krx
        k_ref[rows, h * LANE:(h + 1) * LANE] = _head_norm_rope(slab, tk).astype(BF16)
    v_ref[rows, :] = kvx[:, hw:2 * hw].astype(BF16)


def _proj_kernel(x_ref, mod_ref, gn_ref, win_ref, gq_ref, wq_ref, gkv_ref, wkv_ref, tq_ref, tk_ref,
                 p_ref, q_ref, k_ref, v_ref):
    o1, o2, o3 = POOL_WIDTH, POOL_WIDTH + Q_LORA, POOL_WIDTH + Q_LORA + KV_LORA
    for rows in _row_chunks(x_ref, PROJ_ROWS_PER_DOT):
        proj = _dot(_modulated(x_ref[rows, :], mod_ref, gn_ref), win_ref[...])
        p_ref[rows, :] = proj[:, :o1]
        qs = _dot(_rms(proj[:, o1:o2], gq_ref[...]).astype(BF16), wq_ref[...])
        tq = tq_ref[rows, :]
        for h in range(MLA_HEADS):
            q_ref[rows, h * LANE:(h + 1) * LANE] = _head_norm_rope(qs[:, h * LANE:(h + 1) * LANE], tq).astype(BF16)
        _keys_values(proj[:, o2:o3], proj[:, o3:o3 + LANE], gkv_ref, wkv_ref, tk_ref[rows, :], k_ref, v_ref, rows)


def _proj_ctx_kernel(x_ref, mod_ref, gn_ref, win_ref, gkv_ref, wkv_ref, tk_ref, k_ref, v_ref):
    for rows in _row_chunks(x_ref):
        proj = _dot(_modulated(x_ref[rows, :], mod_ref, gn_ref), win_ref[...])
        _keys_values(proj[:, :KV_LORA], proj[:, KV_LORA:KV_LORA + LANE], gkv_ref, wkv_ref, tk_ref[...], k_ref, v_ref,
                     rows)


def _full(shape):
    return pl.BlockSpec(shape, lambda *_: (0,) * len(shape))


def _proj(x2, mod, gn, win, gq, wq, gkv, wkv, tq, tk, seq, tm):
    t, d = x2.shape
    nps = seq // tm
    hw = MLA_HEADS * LANE
    row = lambda w: pl.BlockSpec((tm, w), lambda i: (i, 0))
    tab = pl.BlockSpec((tm, LANE), lambda i: (i % nps, 0))
    return pl.pallas_call(
        _proj_kernel,
        grid=(t // tm,),
        in_specs=[row(d), pl.BlockSpec((1, N_MOD_ROWS, d), lambda i: (i // nps, 0, 0)), _full(gn.shape),
                  _full(win.shape), _full(gq.shape), _full(wq.shape), _full(gkv.shape), _full(wkv.shape),
                  tab, tab],
        out_specs=[row(POOL_WIDTH), row(hw), row(hw), row(hw)],
        out_shape=[jax.ShapeDtypeStruct((t, POOL_WIDTH), F32), jax.ShapeDtypeStruct((t, hw), BF16),
                   jax.ShapeDtypeStruct((t, hw), BF16), jax.ShapeDtypeStruct((t, hw), BF16)],
        compiler_params=_params("arbitrary"),
        name="proj",
    )(x2, mod, gn, win, gq, wq, gkv, wkv, tq, tk)


def _proj_ctx(c2, modc, gn, win, gkv, wkv, tk, tm):
    t, d = c2.shape
    hw = MLA_HEADS * LANE
    row = lambda w: pl.BlockSpec((tm, w), lambda i: (i, 0))
    return pl.pallas_call(
        _proj_ctx_kernel,
        grid=(t // tm,),
        in_specs=[row(d), _full(modc.shape), _full(gn.shape), _full(win.shape), _full(gkv.shape),
                  _full(wkv.shape), _full(tk.shape)],
        out_specs=[row(hw), row(hw)],
        out_shape=[jax.ShapeDtypeStruct((t, hw), BF16), jax.ShapeDtypeStruct((t, hw), BF16)],
        compiler_params=_params("arbitrary"),
        name="proj_ctx",
    )(c2, modc, gn, win, gkv, wkv, tk)


def _pool_kernel(p_ref, w_ref, sc_ref, o_ref):
    n = p_ref.shape[1]
    t = lax.broadcasted_iota(jnp.int32, (n, POOL_GROUP), 0)

    def shifted(a, d):
        r = pltpu.roll(a, (-d) % n, axis=0)
        return jnp.where((t + d >= 0) & (t + d < n), r, 0.0)

    for g, w in enumerate(POOL_WINDOWS):
        half = w // 2
        sl = slice(g * POOL_GROUP, (g + 1) * POOL_GROUP)
        pg = p_ref[0, :, sl]
        fwd, bwd, k = pg, shifted(pg, -1), 1
        while k < half:
            fwd = fwd + shifted(fwd, k)
            bwd = bwd + shifted(bwd, -k)
            k *= 2
        cnt = (jnp.minimum(t + half, n) - jnp.maximum(t - half, 0)).astype(F32)
        diff = (fwd + bwd) / cnt - pg
        y = _dot(diff.astype(BF16), w_ref[g].astype(BF16)) * sc_ref[:, sl]
        o_ref[0, :, sl] = y.astype(BF16)


def _pool(p3, pool_w, pool_scale):
    b, s, c = p3.shape
    return pl.pallas_call(
        _pool_kernel,
        grid=(b,),
        in_specs=[pl.BlockSpec((1, s, c), lambda i: (i, 0, 0)), _full(pool_w.shape), _full((1, c))],
        out_specs=pl.BlockSpec((1, s, c), lambda i: (i, 0, 0)),
        out_shape=jax.ShapeDtypeStruct((b, s, c), BF16),
        compiler_params=_params("arbitrary"),
        name="pool",
    )(p3, pool_w, pool_scale.reshape(1, c))


def _attn_kernel(q_ref, kl_ref, kc_ref, vl_ref, vc_ref, o_ref):
    nt = (((1,), (1,)), ((), ()))
    blocks = [(kl_ref, vl_ref, slice(k0, k0 + KEY_BLOCK)) for k0 in range(0, kl_ref.shape[0], KEY_BLOCK)]
    blocks.append((kc_ref, vc_ref, slice(0, kc_ref.shape[0])))
    for rows in _row_chunks(q_ref):
        q = q_ref[rows, :]
        m = den = acc = None
        for k_ref, v_ref, ks in blocks:
            s = lax.dot_general(q, k_ref[ks, :], nt, preferred_element_type=F32)
            block_max = jnp.max(s, axis=-1, keepdims=True)
            m_new = block_max if m is None else jnp.maximum(m, block_max)
            p = jnp.exp2(s - m_new)
            pv = _dot(p.astype(BF16), v_ref[ks, :])
            psum = jnp.sum(p, axis=-1, keepdims=True)
            if m is None:
                den, acc = psum, pv
            else:
                alpha = jnp.exp2(m - m_new)
                den, acc = alpha * den + psum, alpha * acc + pv
            m = m_new
        o_ref[rows, :] = (acc / den).astype(BF16)


def _attn(q, kl, kc, vl, vc, batch, seq, nctx, tq):
    t, hw = q.shape
    nq = seq // tq
    return pl.pallas_call(
        _attn_kernel,
        grid=(batch, MLA_HEADS, nq),
        in_specs=[pl.BlockSpec((tq, LANE), lambda b, h, i: (b * nq + i, h)),
                  pl.BlockSpec((seq, LANE), lambda b, h, i: (b, h)),
                  pl.BlockSpec((nctx, LANE), lambda b, h, i: (b, h)),
                  pl.BlockSpec((seq, LANE), lambda b, h, i: (b, h)),
                  pl.BlockSpec((nctx, LANE), lambda b, h, i: (b, h))],
        out_specs=pl.BlockSpec((tq, LANE), lambda b, h, i: (b * nq + i, h)),
        out_shape=jax.ShapeDtypeStruct((t, hw), BF16),
        compiler_params=_params("arbitrary", "arbitrary", "arbitrary"),
        name="attn",
    )(q, kl, kc, vl, vc)


def _mix_kernel(pool_ref, mla_ref, wp_ref, wm_ref, x_ref, mod_ref, gn_ref, x1_ref, h2t_ref):
    for rows in _row_chunks(x_ref):
        mix = _dot(pool_ref[rows, :], wp_ref[...]) + _dot(mla_ref[rows, :], wm_ref[...])
        x1 = x_ref[rows, :] + mod_ref[0, 2:3, :] * mix
        x1_ref[rows, :] = x1
        h2 = _rms(x1, gn_ref[...]) * (1.0 + mod_ref[0, 4:5, :]) + mod_ref[0, 3:4, :]
        h2t_ref[:, rows] = h2.T.astype(BF16)


def _mix(pool2, mla2, wp, wm, x2, mod, gn, seq, tm):
    t, d = x2.shape
    nps = seq // tm
    row = lambda w: pl.BlockSpec((tm, w), lambda i: (i, 0))
    return pl.pallas_call(
        _mix_kernel,
        grid=(t // tm,),
        in_specs=[row(pool2.shape[1]), row(mla2.shape[1]), _full(wp.shape), _full(wm.shape), row(d),
                  pl.BlockSpec((1, N_MOD_ROWS, d), lambda i: (i // nps, 0, 0)), _full(gn.shape)],
        out_specs=[row(d), pl.BlockSpec((d, tm), lambda i: (0, i))],
        out_shape=[jax.ShapeDtypeStruct((t, d), F32), jax.ShapeDtypeStruct((d, t), BF16)],
        compiler_params=_params("arbitrary"),
        name="mix",
    )(pool2, mla2, wp, wm, x2, mod, gn)


def _staircase():
    return [(a, b) for a in range(N_RANK) for b in range(N_RANK) if (a + 1) * (b + 1) <= N_RANK]


def _sorting_network(n):
    def merge(lo, hi, r):
        step = r * 2
        if step < hi - lo:
            yield from merge(lo, hi, step)
            yield from merge(lo + r, hi, step)
            yield from ((i, i + r) for i in range(lo + r, hi - r, step))
        else:
            yield (lo, lo + r)

    def sort(lo, hi):
        if hi - lo >= 1:
            mid = lo + (hi - lo) // 2
            yield from sort(lo, mid)
            yield from sort(mid + 1, hi)
            yield from merge(lo, hi, 1)

    return list(sort(0, n - 1))


def _pprep_kernel(h2t_ref, wq_ref, keys_ref, a_ref, cnt_ref, r2_ref, b_ref, qp_scr, top_scr, s_scr):
    qp_scr[...] = _dot(wq_ref[...], h2t_ref[...]).astype(BF16)
    neg = -jnp.inf
    for h in range(PEER_HEADS):
        for p in range(2):
            row0 = (h * 2 + p) * PEER_HALF
            s = _dot(keys_ref[p], qp_scr[row0:row0 + PEER_HALF, :])
            s_scr[p, h] = s
            for c0 in range(0, s.shape[1], LANE):
                cols = slice(c0, c0 + LANE)
                col = [s[g * SUBLANE:(g + 1) * SUBLANE, cols] for g in range(PEER_N_KEYS // SUBLANE)]
                for i, j in _sorting_network(len(col)):
                    col[i], col[j] = jnp.maximum(col[i], col[j]), jnp.minimum(col[i], col[j])
                for r in range(N_RANK):
                    m = jnp.max(col[0], axis=0, keepdims=True)
                    top_scr[p, r, h:h + 1, cols] = m
                    hit = col[0] == m
                    for d in range(N_RANK - 1 - r):
                        col[d] = jnp.where(hit, col[d + 1], col[d])
    v1 = [top_scr[0, r] for r in range(N_RANK)]
    v2 = [top_scr[1, r] for r in range(N_RANK)]
    pairs = _staircase()
    cand = [v1[a] + v2[b] for a, b in pairs]
    work = [[c for c, (a, _) in zip(cand, pairs) if a == r] for r in range(N_RANK)]
    for r in range(PEER_TOPK):
        kth = functools.reduce(jnp.maximum, [row[0] for row in work])
        if r + 1 < PEER_TOPK:
            for row in work:
                hit = row[0] == kth
                row[:] = [jnp.where(hit, nxt, cur) for cur, nxt in zip(row, row[1:] + [neg])]
    cmax = v1[0] + v2[0]
    keep = [c >= kth for c in cand]
    den = sum(jnp.where(k, jnp.exp(c - cmax), 0.0) for k, c in zip(keep, cand))
    inv = math.sqrt(0.5) / den
    width = [sum(jnp.where(k, 1.0, 0.0) for k, (a, _) in zip(keep, pairs) if a == r) for r in range(N_RANK)]
    for h in range(PEER_HEADS):
        hs = slice(h, h + 1)
        for c0 in range(0, s_scr.shape[-1], LANE):
            cols = slice(c0, c0 + LANE)
            s1 = s_scr[0, h, :, cols]
            cnt = jnp.zeros(s1.shape, F32)
            for r in range(N_RANK):
                cnt = jnp.where(s1 == v1[r][hs, cols], width[r][hs, cols], cnt)
            cnt_ref[h, :, cols] = cnt
            a_ref[h, :, cols] = jnp.exp(s1 - v1[0][hs, cols]) * inv[hs, cols]
            s2 = s_scr[1, h, :, cols]
            rank = jnp.full(s2.shape, float(N_RANK), F32)
            for r in range(N_RANK):
                rank = jnp.where(s2 == v2[r][hs, cols], float(r), rank)
            s_scr[0, h, :, cols] = jnp.exp(s2 - v2[0][hs, cols])
            s_scr[1, h, :, cols] = rank
        b_ref[h] = s_scr[0, h].astype(BF16)
        r2_ref[h] = s_scr[1, h].astype(BF16)


def _pprep(h2t, wqt, keys, tn):
    d, t = h2t.shape
    blk = pl.BlockSpec((PEER_HEADS, PEER_N_KEYS, tn), lambda j: (0, 0, j))
    shp = lambda dt: jax.ShapeDtypeStruct((PEER_HEADS, PEER_N_KEYS, t), dt)
    return pl.pallas_call(
        _pprep_kernel,
        grid=(t // tn,),
        in_specs=[pl.BlockSpec((d, tn), lambda j: (0, j)), _full(wqt.shape), _full(keys.shape)],
        out_specs=[blk, blk, blk, blk],
        out_shape=[shp(F32), shp(F32), shp(BF16), shp(BF16)],
        scratch_shapes=[pltpu.VMEM((wqt.shape[0], tn), BF16),
                        pltpu.VMEM((2, N_RANK, PEER_HEADS, tn), F32),
                        pltpu.VMEM((2, PEER_HEADS, PEER_N_KEYS, tn), F32)],
        compiler_params=_params("arbitrary"),
        name="pprep",
    )(h2t, wqt, keys)


def _pdense_kernel(h2t_ref, u_ref, vt_ref, a_ref, cnt_ref, r2_ref, b_ref, x1_ref, mod_ref, o_ref,
                   acc_ref, g_ref, *, n_i1):
    e = pl.program_id(1)

    @pl.when(e == 0)
    def _():
        acc_ref[...] = jnp.zeros_like(acc_ref)

    shape = (PEER_N_KEYS, h2t_ref.shape[1])
    rows = PDENSE_ROWS_PER_DOT // PEER_N_KEYS
    for i in range(n_i1):
        if i % rows == 0:
            act = _dot(u_ref[i * PEER_N_KEYS:(i + rows) * PEER_N_KEYS, :], h2t_ref[...])
        w = None
        for h in range(PEER_HEADS):
            cnt = jnp.broadcast_to(cnt_ref[h, i:i + 1, :], shape).astype(BF16)
            gate = jnp.broadcast_to(a_ref[h, i:i + 1, :], shape).astype(BF16)
            term = jnp.where(r2_ref[h] < cnt, b_ref[h], jnp.zeros((), BF16)) * gate
            w = term if w is None else w + term
        z = act[(i % rows) * PEER_N_KEYS:(i % rows + 1) * PEER_N_KEYS, :]
        gelu = z * (1.0 + lax.erf(z))
        g_ref[i * PEER_N_KEYS:(i + 1) * PEER_N_KEYS, :] = gelu.astype(BF16) * w
    acc_ref[...] += _dot(vt_ref[...], g_ref[...])

    @pl.when(e == pl.num_programs(1) - 1)
    def _():
        o_ref[...] = x1_ref[...] + mod_ref[0, 5:6, :] * acc_ref[...].T


def _pdense(h2t, u, vt, a, cnt, r2, b, x1, mod, seq, tn, n_i1):
    d, t = h2t.shape
    ne = u.shape[0]
    eb = n_i1 * PEER_N_KEYS
    nps = seq // tn
    gate = pl.BlockSpec((PEER_HEADS, n_i1, tn), lambda j, e: (0, e, j))
    full = pl.BlockSpec((PEER_HEADS, PEER_N_KEYS, tn), lambda j, e: (0, 0, j))
    return pl.pallas_call(
        functools.partial(_pdense_kernel, n_i1=n_i1),
        grid=(t // tn, ne // eb),
        in_specs=[pl.BlockSpec((d, tn), lambda j, e: (0, j)),
                  pl.BlockSpec((eb, d), lambda j, e: (e, 0)),
                  pl.BlockSpec((d, eb), lambda j, e: (0, e)),
                  gate, gate, full, full,
                  pl.BlockSpec((tn, d), lambda j, e: (j, 0)),
                  pl.BlockSpec((1, N_MOD_ROWS, d), lambda j, e: (j // nps, 0, 0))],
        out_specs=pl.BlockSpec((tn, d), lambda j, e: (j, 0)),
        out_shape=jax.ShapeDtypeStruct((t, d), F32),
        scratch_shapes=[pltpu.VMEM((d, tn), F32), pltpu.VMEM((eb, tn), BF16)],
        compiler_params=_params("arbitrary", "arbitrary", vmem=PDENSE_VMEM_LIMIT),
        name="pdense",
    )(h2t, u, vt, a, cnt, r2, b, x1, mod)


def _rot_matrix():
    n = QK_ROPE // 4
    r = np.zeros((QK_ROPE, QK_ROPE), np.float32)
    for axis in range(2):
        for i in range(n):
            r[axis * 2 * n + n + i, axis * 2 * n + i] = -1.0
            r[axis * 2 * n + i, axis * 2 * n + n + i] = 1.0
    return jnp.asarray(r)


def _rope_angles(seq):
    pos = jnp.arange(seq)
    row = (pos // GRID_W).astype(F32)
    col = (pos % GRID_W).astype(F32)
    n = QK_ROPE // 4
    inv = 1.0 / (ROPE_THETA ** (jnp.arange(n, dtype=F32) / n))
    ang_r = row[:, None] * inv
    ang_c = col[:, None] * inv
    return jnp.concatenate([ang_r, ang_r, ang_c, ang_c], axis=-1)


def _head_table(gain, cos, sin, scale):
    rows = cos.shape[0]
    nope = jnp.broadcast_to(gain[:QK_NOPE], (rows, QK_NOPE))
    return jnp.concatenate([nope, gain[QK_NOPE:] * cos, sin], axis=-1) * scale


def _slab_weights(w, gain, width):
    fan_in = w.shape[0]
    wh = w.reshape(fan_in, MLA_HEADS, width)
    nope, rope = wh[..., :QK_NOPE], wh[..., QK_NOPE:QK_HEAD]
    rot = jnp.einsum('khr,rs->khs', rope * gain[QK_NOPE:], _rot_matrix())
    return jnp.concatenate([nope, rope, rot], axis=-1).reshape(fan_in, MLA_HEADS * LANE)


def kernel(x, c, ctx, c_ctx, w_ada, b_ada, g_norm1, w_in, pool_w, pool_scale, g_q_lora, w_q_up, g_kv_lora,
           w_kv_up, g_qk_q, g_qk_k, w_out, g_norm2, peer_w_q, peer_sub_keys, peer_u, peer_v):
    bsz, seq, d = x.shape
    nctx = ctx.shape[1]
    t = bsz * seq
    o1, o2, o3 = POOL_WIDTH, POOL_WIDTH + Q_LORA, POOL_WIDTH + Q_LORA + KV_LORA

    rows = -(-(bsz + 1) // 16) * 16
    cvec = jnp.zeros((rows, d), F32).at[:bsz].set(c).at[bsz].set(c_ctx)
    mod_all = _ada(cvec, w_ada, b_ada).reshape(rows, 6, d)
    mod_all = jnp.pad(mod_all, ((0, 0), (0, N_MOD_ROWS - 6), (0, 0)))
    mod, modc = mod_all[:bsz], mod_all[bsz:bsz + 1]

    zeros = lambda r, cdim: jnp.zeros((r, cdim), F32)
    w_kr = w_in[:, o3:]
    w_krx = jnp.concatenate([zeros(d, QK_NOPE), w_kr, (w_kr * g_qk_k[QK_NOPE:]) @ _rot_matrix()], axis=-1)
    win = jnp.concatenate([w_in[:, :o3], w_krx], axis=-1).astype(BF16)
    win_ctx = jnp.concatenate([w_in[:, o2:o3], w_krx], axis=-1).astype(BF16)
    wq = _slab_weights(w_q_up, g_qk_q, QK_HEAD).astype(BF16)
    wkv_h = w_kv_up.reshape(KV_LORA, MLA_HEADS, QK_NOPE + V_HEAD)
    pad = jnp.zeros((KV_LORA, MLA_HEADS, LANE - QK_NOPE), F32)
    wk = jnp.concatenate([wkv_h[..., :QK_NOPE], pad], axis=-1).reshape(KV_LORA, MLA_HEADS * LANE)
    wv = jnp.concatenate([wkv_h[..., QK_NOPE:], pad], axis=-1).reshape(KV_LORA, MLA_HEADS * LANE)
    wkv = jnp.concatenate([wk, wv], axis=-1).astype(BF16)
    w_out_mla = w_out[POOL_WIDTH:].reshape(MLA_HEADS, V_HEAD, d)
    w_out_mla = jnp.pad(w_out_mla, ((0, 0), (0, LANE - V_HEAD), (0, 0))).reshape(MLA_HEADS * LANE, d).astype(BF16)
    w_out_pool = w_out[:POOL_WIDTH].astype(BF16)

    ang = _rope_angles(seq)
    tq = _head_table(g_qk_q, jnp.cos(ang), jnp.sin(ang), math.log2(math.e) / math.sqrt(QK_HEAD))
    tk = _head_table(g_qk_k, jnp.cos(ang), jnp.sin(ang), 1.0)
    tk_ctx = _head_table(g_qk_k, jnp.ones((8, QK_ROPE), F32), jnp.zeros((8, QK_ROPE), F32), 1.0)[:1]

    row = lambda v: v.reshape(1, -1)
    x2 = x.reshape(t, d)
    tm, tr = min(TOKEN_TILE, seq), min(ROW_TILE, seq)
    p_in, q, k_lat, v_lat = _proj(x2, mod, row(g_norm1), win, row(g_q_lora), wq, row(g_kv_lora), wkv,
                                  tq, tk, seq, tr)
    k_ctx, v_ctx = _proj_ctx(ctx.reshape(bsz * nctx, d), modc, row(g_norm1), win_ctx, row(g_kv_lora), wkv,
                             tk_ctx, nctx)
    pool_out = _pool(p_in.reshape(bsz, seq, POOL_WIDTH), pool_w, pool_scale)
    mla_out = _attn(q, k_lat, k_ctx, v_lat, v_ctx, bsz, seq, nctx, seq)
    x1, h2t = _mix(pool_out.reshape(t, POOL_WIDTH), mla_out, w_out_pool, w_out_mla, x2, mod, row(g_norm2),
                   seq, tr)

    a, cnt, r2, b = _pprep(h2t, peer_w_q.T.astype(BF16), peer_sub_keys.astype(BF16), PPREP_TOKEN_TILE)
    u_scaled = (peer_u * math.sqrt(0.5)).astype(BF16)
    out = _pdense(h2t, u_scaled, peer_v.T.astype(BF16), a, cnt, r2, b, x1, mod, seq, tm, PDENSE_KEYS_PER_STEP)
    return out.reshape(bsz, seq, d)
```
